```python
import math
import jax, jax.numpy as jnp
from jax import lax
import numpy as np

D_MODEL = 2048
BATCH = 4
SEQ = 4096
DEPTH = 2

GRID_W = 64
CTX_LEN = 256
D_MIX = D_MODEL
D_CONV = D_MIX // 4
D_ATTN = D_MIX // 2
D_POOL = D_MIX // 4
N_HEADS = 8
V_DIM = D_ATTN // N_HEADS
QK_DIM = V_DIM // 2
ATTN_SCALE = QK_DIM ** -0.5
Q_BLOCK = 128
CONV_W = 3
POOL_WINDOWS = (2, 4, 8, 16)
N_POOL_GROUPS = 4
POOL_GROUP = D_POOL // N_POOL_GROUPS
ROPE_THETA = 10000.0
LN_EPS = 1e-5
RMS_EPS = 1e-5
DEEPNORM_ALPHA = (2 * DEPTH) ** 0.25
DEEPNORM_BETA = (8 * DEPTH) ** -0.25
SPLIT_SIZES = (D_CONV, D_CONV, D_CONV, D_CONV, D_ATTN, D_ATTN, D_ATTN, D_ATTN, D_POOL, D_POOL)
SPLIT_POINTS = tuple(int(s) for s in np.cumsum(SPLIT_SIZES)[:-1])
D_IN = sum(SPLIT_SIZES)
ATT_K_OFF = 4 * D_CONV + D_ATTN
ATT_V_END = 4 * D_CONV + 3 * D_ATTN

kernel_name = 'hybrid_parallel_heads_dit_block'


def _layernorm(x, g, b):
    xf = x.astype(jnp.float32)
    mu = jnp.mean(xf, axis=-1, keepdims=True)
    var = jnp.mean(jnp.square(xf - mu), axis=-1, keepdims=True)
    return ((xf - mu) * lax.rsqrt(var + LN_EPS) * g + b).astype(x.dtype)


def _qk_heads(t):
    return t.reshape(t.shape[0], t.shape[1], N_HEADS, 2, QK_DIM)


def _v_heads(t):
    return t.reshape(t.shape[0], t.shape[1], N_HEADS, V_DIM)


def _rope_axial(x, rows, cols):
    half = QK_DIM // 2
    n = half // 2
    inv = ROPE_THETA ** (-jnp.arange(n, dtype=jnp.float32) / n)

    def rot(v, pos):
        ang = pos.astype(jnp.float32)[:, None] * inv
        cos = jnp.cos(ang)[None, :, None, None, :]
        sin = jnp.sin(ang)[None, :, None, None, :]
        v1, v2 = v[..., :n], v[..., n:]
        return jnp.concatenate([v1 * cos - v2 * sin, v1 * sin + v2 * cos], axis=-1)

    xf = x.astype(jnp.float32)
    return jnp.concatenate([rot(xf[..., :half], rows), rot(xf[..., half:], cols)], axis=-1).astype(x.dtype)


def _diff_attn(qb, k_all, v_all, lam):
    s = jnp.einsum('bqhid,bkhid->bhiqk', qb.astype(jnp.float32) * ATTN_SCALE, k_all.astype(jnp.float32))
    p = jax.nn.softmax(s, axis=-1)
    a = p[:, :, 0] - lam * p[:, :, 1]
    return jnp.einsum('bhqk,bkhe->bqhe', a.astype(v_all.dtype), v_all)


def _diff_subln(o, g, lam_init):
    of = o.astype(jnp.float32)
    y = of * lax.rsqrt(jnp.mean(jnp.square(of), axis=-1, keepdims=True) + RMS_EPS) * g * (1.0 - lam_init)
    return y.reshape(o.shape[0], o.shape[1], D_ATTN).astype(o.dtype)


def _short_conv(u, b_gate, c_gate, g, w):
    v = c_gate * u
    T = u.shape[1]
    vp = jnp.pad(v, ((0, 0), (1, 1), (0, 0)))
    y = vp[:, :T] * w[0] + vp[:, 1:T + 1] * w[1] + vp[:, 2:] * w[2]
    return jax.nn.silu(g) * (b_gate * y)


def _multiscale_pool(u, g, pool_w_l, pool_scale_l):
    B, T, _ = u.shape
    ug = u.reshape(B, T, N_POOL_GROUPS, POOL_GROUP).astype(jnp.float32)
    csum = jnp.concatenate([jnp.zeros_like(ug[:, :1]), jnp.cumsum(ug, axis=1)], axis=1)
    t = jnp.arange(T)
    diffs = []
    for gi, w in enumerate(POOL_WINDOWS):
        lo = jnp.clip(t - w // 2, 0, T)
        hi = jnp.clip(t + w - w // 2, 0, T)
        mean = (csum[:, hi, gi] - csum[:, lo, gi]) / (hi - lo).astype(jnp.float32)[None, :, None]
        diffs.append(mean - ug[:, :, gi])
    d = jnp.stack(diffs, axis=2).astype(u.dtype)
    y = jnp.einsum('btgc,gcd->btgd', d, pool_w_l).reshape(B, T, D_POOL) * pool_scale_l
    return jax.nn.silu(g) * y


def _mix_output(parts, att, lam_init, conv_w_l, subln_g_l, pool_w_l, pool_scale_l, w_out_l):
    cu, cb, cc, cg, _q, _k, _v, ag, pu, pg = parts
    y_conv = _short_conv(cu, cb, cc, cg, conv_w_l)
    y_attn = jax.nn.silu(ag) * _diff_subln(att, subln_g_l, lam_init)
    y_pool = _multiscale_pool(pu, pg, pool_w_l, pool_scale_l)
    return jnp.concatenate([y_conv, y_attn, y_pool], axis=-1) @ w_out_l


def setup_inputs(seed: int = 0) -> dict:
    key = jax.random.key(seed)
    ks = jax.random.split(key, 18)
    f32 = jnp.float32

    def nrm(k, shape, s):
        return s * jax.random.normal(k, shape, f32)

    return {
        'x': nrm(ks[0], (BATCH, SEQ, D_MODEL), 1.0),
        'c': nrm(ks[1], (BATCH, D_MODEL), 1.0),
        'ctx': nrm(ks[2], (BATCH, CTX_LEN, D_MODEL), 1.0),
        'c_ctx': nrm(ks[3], (D_MODEL,), 1.0),
        'w_mod': nrm(ks[4], (DEPTH, D_MODEL, 3 * D_MODEL), 0.5 * D_MODEL ** -0.5),
        'b_mod': nrm(ks[5], (DEPTH, 3 * D_MODEL), 0.02),
        'w_in': nrm(ks[6], (DEPTH, D_MODEL, D_IN), D_MODEL ** -0.5),
        'conv_w': nrm(ks[7], (DEPTH, CONV_W, D_CONV), CONV_W ** -0.5),
        'lam_q1': nrm(ks[8], (DEPTH, QK_DIM), 0.1),
        'lam_k1': nrm(ks[9], (DEPTH, QK_DIM), 0.1),
        'lam_q2': nrm(ks[10], (DEPTH, QK_DIM), 0.1),
        'lam_k2': nrm(ks[11], (DEPTH, QK_DIM), 0.1),
        'subln_g': 1.0 + nrm(ks[12], (DEPTH, V_DIM), 0.02),
        'pool_w': nrm(ks[13], (DEPTH, N_POOL_GROUPS, POOL_GROUP, POOL_GROUP), POOL_GROUP ** -0.5),
        'pool_scale': 1.0 + nrm(ks[14], (DEPTH, D_POOL), 0.02),
        'w_out': nrm(ks[15], (DEPTH, D_MIX, D_MODEL), DEEPNORM_BETA * D_MIX ** -0.5),
        'ln_g': 1.0 + nrm(ks[16], (DEPTH, D_MODEL), 0.02),
        'ln_b': nrm(ks[17], (DEPTH, D_MODEL), 0.02),
    }


def reference(x, c, ctx, c_ctx, w_mod, b_mod, w_in, conv_w, lam_q1, lam_k1, lam_q2, lam_k2,
              subln_g, pool_w, pool_scale, w_out, ln_g, ln_b):
    B, L, _ = x.shape
    ROWS = L // GRID_W
    rows = jnp.repeat(jnp.arange(ROWS, dtype=jnp.int32), GRID_W)
    cols = jnp.broadcast_to(jnp.arange(GRID_W, dtype=jnp.int32)[None, :], (ROWS, GRID_W)).reshape(-1)
    n_blk = L // Q_BLOCK
    xl, xc = x, ctx
    for l in range(DEPTH):
        last = l == DEPTH - 1
        sh_l, sc_l, g_l = jnp.split(jax.nn.silu(c) @ w_mod[l] + b_mod[l], 3, axis=-1)
        sh_c, sc_c, g_c = jnp.split(jax.nn.silu(c_ctx) @ w_mod[l] + b_mod[l], 3, axis=-1)
        hl = xl * (1.0 + sc_l[:, None, :]) + sh_l[:, None, :]
        hc = xc * (1.0 + sc_c) + sh_c
        lam_init = 0.8 - 0.6 * math.exp(-0.3 * l)
        lam = (jnp.exp(jnp.sum(lam_q1[l].astype(jnp.float32) * lam_k1[l].astype(jnp.float32)))
               - jnp.exp(jnp.sum(lam_q2[l].astype(jnp.float32) * lam_k2[l].astype(jnp.float32)))
               + lam_init)
        parts_l = jnp.split(hl @ w_in[l], SPLIT_POINTS, axis=-1)
        ql = _rope_axial(_qk_heads(parts_l[4]), rows, cols)
        kl = _rope_axial(_qk_heads(parts_l[5]), rows, cols)
        vl = _v_heads(parts_l[6])
        if last:
            kc, vc = jnp.split(hc @ w_in[l][:, ATT_K_OFF:ATT_V_END], 2, axis=-1)
            parts_c = None
        else:
            parts_c = jnp.split(hc @ w_in[l], SPLIT_POINTS, axis=-1)
            kc, vc = parts_c[5], parts_c[6]
        kc = _qk_heads(kc)
        vc = _v_heads(vc)
        k_all = jnp.concatenate([kc, kl], axis=1)
        v_all = jnp.concatenate([vc, vl], axis=1)
        q_blocks = ql.reshape(B, n_blk, Q_BLOCK, N_HEADS, 2, QK_DIM).swapaxes(0, 1)
        att_l = lax.map(lambda qb: _diff_attn(qb, k_all, v_all, lam), q_blocks)
        att_l = att_l.swapaxes(0, 1).reshape(B, L, N_HEADS, V_DIM)
        out_l = _mix_output(parts_l, att_l, lam_init, conv_w[l], subln_g[l], pool_w[l], pool_scale[l], w_out[l])
        xl_new = _layernorm(DEEPNORM_ALPHA * xl + g_l[:, None, :] * out_l, ln_g[l], ln_b[l])
        if not last:
            att_c = _diff_attn(_qk_heads(parts_c[4]), kc, vc, lam)
            out_c = _mix_output(parts_c, att_c, lam_init, conv_w[l], subln_g[l], pool_w[l], pool_scale[l], w_out[l])
            xc = _layernorm(DEEPNORM_ALPHA * xc + g_c * out_c, ln_g[l], ln_b[l])
        xl = xl_new
    return xl
```

```python
import functools
import math

import numpy as np
import jax
import jax.numpy as jnp
from jax import lax
from jax.experimental import pallas as pl
from jax.experimental.pallas import tpu as pltpu

F32 = jnp.float32
BF16 = jnp.bfloat16

D_MODEL = 2048
DEPTH = 2
GRID_W = 64
D_CONV = 512
D_ATTN = 1024
D_POOL = 512
N_HEADS = 8
V_DIM = 128
QK_DIM = 64
ATTN_SCALE = QK_DIM ** -0.5
POOL_WINDOWS = (2, 4, 8, 16)
POOL_GROUP = 128
ROPE_THETA = 10000.0
LN_EPS = 1e-5
RMS_EPS = 1e-5
DEEPNORM_ALPHA = (2 * DEPTH) ** 0.25
D_IN = 4 * D_CONV + 4 * D_ATTN + 2 * D_POOL
Q_OFF = 4 * D_CONV
K_OFF = Q_OFF + D_ATTN
V_OFF = K_OFF + D_ATTN
G_OFF = V_OFF + D_ATTN
POOL_OFF = G_OFF + D_ATTN

LANES = 128
BF16_ROWS = 16
HALO = BF16_ROWS
MOD_ROWS = 8
VMEM_LIMIT = 56 * 1024 * 1024

PROJ_TM = 1024
PROJ_TN = 1024
ATTN_TQ = 256
MIX_TM = 256
MOD_TN = 1536


def _silu(x):
    return x / (1.0 + jnp.exp(-x))


def _params(*sem):
    return pltpu.CompilerParams(dimension_semantics=sem, vmem_limit_bytes=VMEM_LIMIT)


def _mod_kernel(c_ref, w_ref, b_ref, lamv_ref, mod_ref, lam_ref):
    layer = pl.program_id(0)
    s = _silu(c_ref[...]).astype(BF16)
    w = w_ref[0].astype(BF16)
    mod_ref[0] = jnp.dot(s, w, preferred_element_type=F32) + b_ref[0]
    lv = lamv_ref[0]
    d1 = jnp.sum(lv[0:1] * lv[1:2], axis=-1, keepdims=True)
    d2 = jnp.sum(lv[2:3] * lv[3:4], axis=-1, keepdims=True)
    lf = jnp.zeros((1, LANES), F32) + layer.astype(F32)
    lam_init = 0.8 - 0.6 * jnp.exp(-0.3 * lf)
    lam_ref[0] = jnp.exp(d1) - jnp.exp(d2) + lam_init


def _modulation(cvec, w_mod, b_mod, lamv):
    n_out = w_mod.shape[-1]
    return pl.pallas_call(
        _mod_kernel,
        grid=(DEPTH, n_out // MOD_TN),
        in_specs=[
            pl.BlockSpec((MOD_ROWS, D_MODEL), lambda l, j: (0, 0)),
            pl.BlockSpec((1, D_MODEL, MOD_TN), lambda l, j: (l, 0, j)),
            pl.BlockSpec((1, 1, MOD_TN), lambda l, j: (l, 0, j)),
            pl.BlockSpec((1, 4, QK_DIM), lambda l, j: (l, 0, 0)),
        ],
        out_specs=[
            pl.BlockSpec((1, MOD_ROWS, MOD_TN), lambda l, j: (l, 0, j)),
            pl.BlockSpec((1, 1, LANES), lambda l, j: (l, 0, 0)),
        ],
        out_shape=[
            jax.ShapeDtypeStruct((DEPTH, MOD_ROWS, n_out), F32),
            jax.ShapeDtypeStruct((DEPTH, 1, LANES), F32),
        ],
        compiler_params=_params("arbitrary", "arbitrary"),
        name="modulation",
    )(cvec, w_mod, b_mod.reshape(DEPTH, 1, n_out), lamv)


def _proj_kernel(*refs, rope, q_blk, k_blk):
    if rope:
        x_ref, sh_ref, sc_ref, w_ref, cos_ref, sa_ref, sb_ref, o_ref, h_ref = refs
    else:
        x_ref, sh_ref, sc_ref, w_ref, o_ref, h_ref = refs
    j = pl.program_id(2)

    @pl.when(j == 0)
    def _():
        h_ref[...] = (x_ref[0] * (1.0 + sc_ref[0]) + sh_ref[0]).astype(BF16)

    acc = jnp.dot(h_ref[...], w_ref[...], preferred_element_type=F32)
    o_ref[0] = acc.astype(BF16)

    if rope:
        @pl.when((j == q_blk) | (j == k_blk))
        def _():
            scale = jnp.where(j == q_blk, ATTN_SCALE, 1.0).astype(F32)
            cos = cos_ref[...] * scale
            sa = sa_ref[...] * scale
            sb = sb_ref[...] * scale
            for c in range(acc.shape[1] // LANES):
                xc = acc[:, c * LANES:(c + 1) * LANES]
                r = (xc * cos + pltpu.roll(xc, LANES - QK_DIM // 4, 1) * sa
                     + pltpu.roll(xc, QK_DIM // 4, 1) * sb)
                o_ref[0, :, c * LANES:(c + 1) * LANES] = r.astype(BF16)


def _project(x, mods3, mod_row, w_bf16, col_blk0, n_col_blks, tm, rope_tables):
    nb, t, _ = x.shape
    tn = PROJ_TN
    rope = rope_tables is not None
    in_specs = [
        pl.BlockSpec((1, tm, D_MODEL), lambda b, i, j: (b, i, 0)),
        pl.BlockSpec((1, 1, D_MODEL), lambda b, i, j: (mod_row(b), 0, 0)),
        pl.BlockSpec((1, 1, D_MODEL), lambda b, i, j: (mod_row(b), 0, 1)),
        pl.BlockSpec((D_MODEL, tn), lambda b, i, j: (0, j + col_blk0)),
    ]
    args = [x, mods3, mods3, w_bf16]
    if rope:
        in_specs += [pl.BlockSpec((tm, LANES), lambda b, i, j: (i, 0))] * 3
        args += list(rope_tables)
    return pl.pallas_call(
        functools.partial(_proj_kernel, rope=rope, q_blk=Q_OFF // tn - col_blk0, k_blk=K_OFF // tn - col_blk0),
        grid=(nb, t // tm, n_col_blks),
        in_specs=in_specs,
        out_specs=pl.BlockSpec((1, tm, tn), lambda b, i, j: (b, i, j)),
        out_shape=jax.ShapeDtypeStruct((nb, t, n_col_blks * tn), BF16),
        scratch_shapes=[pltpu.VMEM((tm, D_MODEL), BF16)],
        compiler_params=_params("arbitrary", "arbitrary", "arbitrary"),
        name="proj_rope" if rope else "proj",
    )(*args)


def _attn_kernel(*refs, n_seg, lam_init):
    lam_ref, gain_ref, q_ref, ag_ref = refs[:4]
    kv_refs = refs[4:4 + 2 * n_seg]
    o_ref = refs[4 + 2 * n_seg]
    tq = q_ref.shape[1]
    nt = (((1,), (1,)), ((), ()))

    q = q_ref[0]
    lane = lax.broadcasted_iota(jnp.int32, (1, LANES), 1)
    zero = jnp.zeros_like(q)
    qs = jnp.concatenate([jnp.where(lane < QK_DIM, q, zero), jnp.where(lane >= QK_DIM, q, zero)], axis=0)

    scores = [lax.dot_general(qs, kv_refs[2 * s][0], nt, preferred_element_type=F32) for s in range(n_seg)]
    m = scores[0].max(axis=-1, keepdims=True)
    for s in scores[1:]:
        m = jnp.maximum(m, s.max(axis=-1, keepdims=True))
    probs = [jnp.exp(s - m) for s in scores]
    den = probs[0].sum(axis=-1, keepdims=True)
    for p in probs[1:]:
        den = den + p.sum(axis=-1, keepdims=True)
    den1, den2 = den[:tq], den[tq:]
    lam = lam_ref[:, 0:1]
    ratio = lam * den1 / den2
    o = None
    for s in range(n_seg):
        a = (probs[s][:tq] - ratio * probs[s][tq:]).astype(BF16)
        pv = jnp.dot(a, kv_refs[2 * s + 1][0], preferred_element_type=F32)
        o = pv if o is None else o + pv
    o = o / den1
    y = o * lax.rsqrt(jnp.mean(o * o, axis=-1, keepdims=True) + RMS_EPS) * (gain_ref[...] * (1.0 - lam_init))
    o_ref[0] = (_silu(ag_ref[0].astype(F32)) * y).astype(BF16)


def _attention(lam, gain, p_q, segs, tq, lam_init):
    nb, t_q, _ = p_q.shape
    in_specs = [
        pl.BlockSpec((1, LANES), lambda b, h, i: (0, 0)),
        pl.BlockSpec((1, V_DIM), lambda b, h, i: (0, 0)),
        pl.BlockSpec((1, tq, V_DIM), lambda b, h, i: (b, i, Q_OFF // V_DIM + h)),
        pl.BlockSpec((1, tq, V_DIM), lambda b, h, i: (b, i, G_OFF // V_DIM + h)),
    ]
    args = [lam, gain, p_q, p_q]
    for arr, k_off, v_off in segs:
        t_k = arr.shape[1]
        in_specs.append(pl.BlockSpec((1, t_k, V_DIM), lambda b, h, i, o=k_off // V_DIM: (b, 0, o + h)))
        in_specs.append(pl.BlockSpec((1, t_k, V_DIM), lambda b, h, i, o=v_off // V_DIM: (b, 0, o + h)))
        args += [arr, arr]
    return pl.pallas_call(
        functools.partial(_attn_kernel, n_seg=len(segs), lam_init=lam_init),
        grid=(nb, N_HEADS, t_q // tq),
        in_specs=in_specs,
        out_specs=pl.BlockSpec((1, tq, V_DIM), lambda b, h, i: (b, i, h)),
        out_shape=jax.ShapeDtypeStruct((nb, t_q, D_ATTN), BF16),
        compiler_params=_params("arbitrary", "arbitrary", "arbitrary"),
        name="diff_attn_%dseg" % len(segs),
    )(*args)


def _mix_kernel(pc_ref, cup_ref, ccp_ref, cun_ref, ccn_ref, pp_ref, pup_ref, pun_ref, ya_ref, x_ref, g_ref,
                cw_ref, pw_ref, ps_ref, wo_ref, lg_ref, lb_ref, o_ref, mix_ref, *, seq_len):
    i = pl.program_id(1)
    tm = x_ref.shape[1]
    n_tiles = seq_len // tm
    has_prev = i > 0
    has_next = i < n_tiles - 1
    row = lax.broadcasted_iota(jnp.int32, (tm, 1), 0)

    cu = pc_ref[0, :, 0:D_CONV].astype(F32)
    cb = pc_ref[0, :, D_CONV:2 * D_CONV].astype(F32)
    cc = pc_ref[0, :, 2 * D_CONV:3 * D_CONV].astype(F32)
    cg = pc_ref[0, :, 3 * D_CONV:4 * D_CONV].astype(F32)
    v = cc * cu
    v_before = (ccp_ref[0].astype(F32) * cup_ref[0].astype(F32))[HALO - 1:HALO]
    v_after = (ccn_ref[0].astype(F32) * cun_ref[0].astype(F32))[0:1]
    v_before = jnp.where(has_prev, v_before, 0.0)
    v_after = jnp.where(has_next, v_after, 0.0)
    v_m1 = jnp.where(row == 0, v_before, pltpu.roll(v, 1, 0))
    v_p1 = jnp.where(row == tm - 1, v_after, pltpu.roll(v, tm - 1, 0))
    cw = cw_ref[...]
    y_conv = _silu(cg) * (cb * (v_m1 * cw[0:1] + v * cw[1:2] + v_p1 * cw[2:3]))
    mix_ref[:, 0:D_CONV] = y_conv.astype(BF16)

    mix_ref[:, D_CONV:D_CONV + D_ATTN] = ya_ref[0]

    u = pp_ref[0, :, 0:D_POOL].astype(F32)
    pg = pp_ref[0, :, D_POOL:2 * D_POOL].astype(F32)
    u_before = jnp.where(has_prev, pup_ref[0].astype(F32), 0.0)
    u_after = jnp.where(has_next, pun_ref[0].astype(F32), 0.0)
    u_ext = jnp.concatenate([u_before, u, u_after], axis=0)
    n_ext = tm + 2 * HALO
    t = i * tm + row
    for gi, w in enumerate(POOL_WINDOWS):
        sl = slice(gi * POOL_GROUP, (gi + 1) * POOL_GROUP)
        s = u_ext[:, sl]
        k = 1
        while k < w:
            s = s + pltpu.roll(s, n_ext - k, 0)
            k *= 2
        win = pltpu.roll(s, w // 2, 0)[HALO:HALO + tm]
        lo = jnp.maximum(t - w // 2, 0)
        hi = jnp.minimum(t + w - w // 2, seq_len)
        d = (win / (hi - lo).astype(F32) - u[:, sl]).astype(BF16)
        yg = jnp.dot(d, pw_ref[gi], preferred_element_type=F32)
        y_pool = _silu(pg[:, sl]) * (yg * ps_ref[:, sl])
        mix_ref[:, D_CONV + D_ATTN + gi * POOL_GROUP:D_CONV + D_ATTN + (gi + 1) * POOL_GROUP] = y_pool.astype(BF16)

    out = jnp.dot(mix_ref[...], wo_ref[...], preferred_element_type=F32)
    z = DEEPNORM_ALPHA * x_ref[0] + g_ref[0] * out
    mu = jnp.mean(z, axis=-1, keepdims=True)
    zc = z - mu
    var = jnp.mean(zc * zc, axis=-1, keepdims=True)
    o_ref[0] = zc * lax.rsqrt(var + LN_EPS) * lg_ref[...] + lb_ref[...]


def _mix(p, y_attn, x, mods3, mod_row, conv_w, pool_w_bf16, pool_scale, w_out_bf16, ln_g, ln_b, tm):
    nb, t, _ = x.shape
    hb = tm // HALO
    last = t // HALO - 1
    prev = lambda i: jnp.maximum(i * hb - 1, 0)
    nxt = lambda i: jnp.minimum((i + 1) * hb, last)
    cc_blk = 2
    const2 = lambda b, i: (0, 0)
    in_specs = [
        pl.BlockSpec((1, tm, 4 * D_CONV), lambda b, i: (b, i, 0)),
        pl.BlockSpec((1, HALO, D_CONV), lambda b, i: (b, prev(i), 0)),
        pl.BlockSpec((1, HALO, D_CONV), lambda b, i: (b, prev(i), cc_blk)),
        pl.BlockSpec((1, HALO, D_CONV), lambda b, i: (b, nxt(i), 0)),
        pl.BlockSpec((1, HALO, D_CONV), lambda b, i: (b, nxt(i), cc_blk)),
        pl.BlockSpec((1, tm, 2 * D_POOL), lambda b, i: (b, i, POOL_OFF // (2 * D_POOL))),
        pl.BlockSpec((1, HALO, D_POOL), lambda b, i: (b, prev(i), POOL_OFF // D_POOL)),
        pl.BlockSpec((1, HALO, D_POOL), lambda b, i: (b, nxt(i), POOL_OFF // D_POOL)),
        pl.BlockSpec((1, tm, D_ATTN), lambda b, i: (b, i, 0)),
        pl.BlockSpec((1, tm, D_MODEL), lambda b, i: (b, i, 0)),
        pl.BlockSpec((1, 1, D_MODEL), lambda b, i: (mod_row(b), 0, 2)),
        pl.BlockSpec(conv_w.shape, const2),
        pl.BlockSpec(pool_w_bf16.shape, lambda b, i: (0, 0, 0)),
        pl.BlockSpec(pool_scale.shape, const2),
        pl.BlockSpec(w_out_bf16.shape, const2, pipeline_mode=pl.Buffered(1)),
        pl.BlockSpec(ln_g.shape, const2),
        pl.BlockSpec(ln_b.shape, const2),
    ]
    return pl.pallas_call(
        functools.partial(_mix_kernel, seq_len=t),
        grid=(nb, t // tm),
        in_specs=in_specs,
        out_specs=pl.BlockSpec((1, tm, D_MODEL), lambda b, i: (b, i, 0)),
        out_shape=jax.ShapeDtypeStruct((nb, t, D_MODEL), F32),
        scratch_shapes=[pltpu.VMEM((tm, D_MODEL), BF16)],
        compiler_params=_params("arbitrary", "arbitrary"),
        name="mix_out_ln",
    )(p, p, p, p, p, p, p, p, y_attn, x, mods3, conv_w, pool_w_bf16, pool_scale, w_out_bf16, ln_g, ln_b)


def _rope_tables(seq_len):
    n = QK_DIM // 4
    inv = ROPE_THETA ** (-np.arange(n, dtype=np.float64) / n)
    tok = np.arange(seq_len)
    pos = np.stack([tok // GRID_W, tok % GRID_W], axis=1).astype(np.float64)
    m = np.arange(LANES)
    ang = pos[:, (m % QK_DIM) // (QK_DIM // 2)] * inv[m % n][None, :]
    second = ((m % (QK_DIM // 2)) >= n)[None, :]
    cos, sin = np.cos(ang), np.sin(ang)
    sa = np.where(second, 0.0, -sin)
    sb = np.where(second, sin, 0.0)
    return tuple(jnp.asarray(a, dtype=F32) for a in (cos, sa, sb))


def kernel(x, c, ctx, c_ctx, w_mod, b_mod, w_in, conv_w, lam_q1, lam_k1, lam_q2, lam_k2, subln_g, pool_w,
           pool_scale, w_out, ln_g, ln_b):
    nb, seq, _ = x.shape
    ctx_len = ctx.shape[1]
    assert nb + 1 <= MOD_ROWS

    cvec = jnp.concatenate([c, c_ctx[None, :], jnp.zeros((MOD_ROWS - nb - 1, D_MODEL), F32)], axis=0)
    lamv = jnp.stack([lam_q1, lam_k1, lam_q2, lam_k2], axis=1).astype(F32)
    mods, lams = _modulation(cvec, w_mod, b_mod, lamv)
    tables = _rope_tables(seq)
    latent_row = lambda b: b
    ctx_row = lambda b: nb

    xl, xc = x, ctx
    for l in range(DEPTH):
        last = l == DEPTH - 1
        lam_init = 0.8 - 0.6 * math.exp(-0.3 * l)
        mods3 = mods[l].reshape(MOD_ROWS, 1, 3 * D_MODEL)
        w_in_l = w_in[l].astype(BF16)
        w_out_l = w_out[l].astype(BF16)
        pool_w_l = pool_w[l].astype(BF16)
        gain = subln_g[l].reshape(1, V_DIM)
        p_l = _project(xl, mods3, latent_row, w_in_l, 0, D_IN // PROJ_TN, PROJ_TM, tables)
        xc_flat = xc.reshape(1, nb * ctx_len, D_MODEL)
        if last:
            p_c = _project(xc_flat, mods3, ctx_row, w_in_l, K_OFF // PROJ_TN, 2 * D_ATTN // PROJ_TN,
                           nb * ctx_len, None).reshape(nb, ctx_len, 2 * D_ATTN)
            ctx_seg = (p_c, 0, D_ATTN)
        else:
            p_c = _project(xc_flat, mods3, ctx_row, w_in_l, 0, D_IN // PROJ_TN, nb * ctx_len,
                           None).reshape(nb, ctx_len, D_IN)
            ctx_seg = (p_c, K_OFF, V_OFF)
        y_attn = _attention(lams[l], gain, p_l, [ctx_seg, (p_l, K_OFF, V_OFF)], ATTN_TQ, lam_init)
        mix_args = (conv_w[l], pool_w_l, pool_scale[l].reshape(1, D_POOL), w_out_l,
                    ln_g[l].reshape(1, D_MODEL), ln_b[l].reshape(1, D_MODEL))
        xl_new = _mix(p_l, y_attn, xl, mods3, latent_row, *mix_args, MIX_TM)
        if not last:
            y_attn_c = _attention(lams[l], gain, p_c, [ctx_seg], ctx_len, lam_init)
            xc = _mix(p_c, y_attn_c, xc, mods3, ctx_row, *mix_args, ctx_len)
        xl = xl_new
    return xl
```

```python
import functools
import math

import numpy as np
import jax
import jax.numpy as jnp
from jax import lax
from jax.experimental import pallas as pl
from jax.experimental.pallas import tpu as pltpu

F32 = jnp.float32
BF16 = jnp.bfloat16

D_MODEL = 2048
DEPTH = 2
GRID_W = 64
D_CONV = 512
D_ATTN = 1024
D_POOL = 512
N_HEADS = 8
V_DIM = 128
QK_DIM = 64
ATTN_SCALE = QK_DIM ** -0.5
LOG2_E = math.log2(math.e)
Q_PRESCALE = ATTN_SCALE * LOG2_E
POOL_WINDOWS = (2, 4, 8, 16)
POOL_GROUP = 128
ROPE_THETA = 10000.0
LN_EPS = 1e-5
RMS_EPS = 1e-5
DEEPNORM_ALPHA = (2 * DEPTH) ** 0.25
D_IN = 4 * D_CONV + 4 * D_ATTN + 2 * D_POOL
Q_OFF = 4 * D_CONV
K_OFF = Q_OFF + D_ATTN
V_OFF = K_OFF + D_ATTN
G_OFF = V_OFF + D_ATTN
POOL_OFF = G_OFF + D_ATTN

LANES = 128
BF16_ROWS = 16
HALO = BF16_ROWS
MOD_ROWS = 8
VMEM_LIMIT = 56 * 1024 * 1024

PROJ_TM = 1024
PROJ_TN = 1024
ATTN_TQ = 256
ATTN_CHUNK = 256
ATTN_FLAGS = None
MIX_TM = 256
MOD_TN = 1536


def _silu(x):
    return x / (1.0 + jnp.exp(-x))


def _params(*sem, flags=None):
    return pltpu.CompilerParams(dimension_semantics=sem, vmem_limit_bytes=VMEM_LIMIT, flags=flags)


def _mod_kernel(c_ref, w_ref, b_ref, lamv_ref, mod_ref, lam_ref):
    layer = pl.program_id(0)
    s = _silu(c_ref[...]).astype(BF16)
    w = w_ref[0].astype(BF16)
    mod_ref[0] = jnp.dot(s, w, preferred_element_type=F32) + b_ref[0]
    lv = lamv_ref[0]
    d1 = jnp.sum(lv[0:1] * lv[1:2], axis=-1, keepdims=True)
    d2 = jnp.sum(lv[2:3] * lv[3:4], axis=-1, keepdims=True)
    lf = jnp.zeros((1, LANES), F32) + layer.astype(F32)
    lam_init = 0.8 - 0.6 * jnp.exp(-0.3 * lf)
    lam_ref[0] = jnp.exp(d1) - jnp.exp(d2) + lam_init


def _modulation(cvec, w_mod, b_mod, lamv):
    n_out = w_mod.shape[-1]
    return pl.pallas_call(
        _mod_kernel,
        grid=(DEPTH, n_out // MOD_TN),
        in_specs=[
            pl.BlockSpec((MOD_ROWS, D_MODEL), lambda l, j: (0, 0)),
            pl.BlockSpec((1, D_MODEL, MOD_TN), lambda l, j: (l, 0, j)),
            pl.BlockSpec((1, 1, MOD_TN), lambda l, j: (l, 0, j)),
            pl.BlockSpec((1, 4, QK_DIM), lambda l, j: (l, 0, 0)),
        ],
        out_specs=[
            pl.BlockSpec((1, MOD_ROWS, MOD_TN), lambda l, j: (l, 0, j)),
            pl.BlockSpec((1, 1, LANES), lambda l, j: (l, 0, 0)),
        ],
        out_shape=[
            jax.ShapeDtypeStruct((DEPTH, MOD_ROWS, n_out), F32),
            jax.ShapeDtypeStruct((DEPTH, 1, LANES), F32),
        ],
        compiler_params=_params("arbitrary", "arbitrary"),
        name="modulation",
    )(cvec, w_mod, b_mod.reshape(DEPTH, 1, n_out), lamv)


def _proj_kernel(*refs, rope, q_blk, k_blk):
    if rope:
        x_ref, sh_ref, sc_ref, w_ref, cos_ref, sa_ref, sb_ref, o_ref, h_ref = refs
    else:
        x_ref, sh_ref, sc_ref, w_ref, o_ref, h_ref = refs
    j = pl.program_id(2)

    @pl.when(j == 0)
    def _():
        h_ref[...] = (x_ref[0] * (1.0 + sc_ref[0]) + sh_ref[0]).astype(BF16)

    acc = jnp.dot(h_ref[...], w_ref[...], preferred_element_type=F32)
    o_ref[0] = acc.astype(BF16)

    if rope:
        @pl.when((j == q_blk) | (j == k_blk))
        def _():
            scale = jnp.where(j == q_blk, Q_PRESCALE, 1.0).astype(F32)
            cos = cos_ref[...] * scale
            sa = sa_ref[...] * scale
            sb = sb_ref[...] * scale
            for c in range(acc.shape[1] // LANES):
                xc = acc[:, c * LANES:(c + 1) * LANES]
                r = (xc * cos + pltpu.roll(xc, LANES - QK_DIM // 4, 1) * sa
                     + pltpu.roll(xc, QK_DIM // 4, 1) * sb)
                o_ref[0, :, c * LANES:(c + 1) * LANES] = r.astype(BF16)


def _project(x, mods3, mod_row, w_bf16, col_blk0, n_col_blks, tm, rope_tables):
    nb, t, _ = x.shape
    tn = PROJ_TN
    rope = rope_tables is not None
    in_specs = [
        pl.BlockSpec((1, tm, D_MODEL), lambda b, i, j: (b, i, 0)),
        pl.BlockSpec((1, 1, D_MODEL), lambda b, i, j: (mod_row(b), 0, 0)),
        pl.BlockSpec((1, 1, D_MODEL), lambda b, i, j: (mod_row(b), 0, 1)),
        pl.BlockSpec((D_MODEL, tn), lambda b, i, j: (0, j + col_blk0)),
    ]
    args = [x, mods3, mods3, w_bf16]
    if rope:
        in_specs += [pl.BlockSpec((tm, LANES), lambda b, i, j: (i, 0))] * 3
        args += list(rope_tables)
    return pl.pallas_call(
        functools.partial(_proj_kernel, rope=rope, q_blk=Q_OFF // tn - col_blk0, k_blk=K_OFF // tn - col_blk0),
        grid=(nb, t // tm, n_col_blks),
        in_specs=in_specs,
        out_specs=pl.BlockSpec((1, tm, tn), lambda b, i, j: (b, i, j)),
        out_shape=jax.ShapeDtypeStruct((nb, t, n_col_blks * tn), BF16),
        scratch_shapes=[pltpu.VMEM((tm, D_MODEL), BF16)],
        compiler_params=_params("arbitrary", "arbitrary", "arbitrary"),
        name="proj_rope" if rope else "proj",
    )(*args)


def _split_halves(q):
    lane = lax.broadcasted_iota(jnp.int32, (1, LANES), 1)
    zero = jnp.zeros_like(q)
    return jnp.concatenate([jnp.where(lane < QK_DIM, q, zero), jnp.where(lane >= QK_DIM, q, zero)], axis=0)


def _subln_gate(o, gain_ref, ag_ref, lam_init):
    y = o * lax.rsqrt(jnp.mean(o * o, axis=-1, keepdims=True) + RMS_EPS) * (gain_ref[...] * (1.0 - lam_init))
    return (_silu(ag_ref[0].astype(F32)) * y).astype(BF16)


def _attn_ctx_kernel(lam_ref, gain_ref, q_ref, ag_ref, k_ref, v_ref, o_ref, *, lam_init):
    tq = q_ref.shape[1]
    qs = _split_halves(q_ref[0] * ATTN_SCALE)
    s = lax.dot_general(qs, k_ref[0], (((1,), (1,)), ((), ())), preferred_element_type=F32)
    p = jnp.exp(s - s.max(axis=-1, keepdims=True))
    den = p.sum(axis=-1, keepdims=True)
    den1, den2 = den[:tq], den[tq:]
    ratio = lam_ref[:, 0:1] * den1 / den2
    a = (p[:tq] - ratio * p[tq:]).astype(BF16)
    o = jnp.dot(a, v_ref[0], preferred_element_type=F32) / den1
    o_ref[0] = _subln_gate(o, gain_ref, ag_ref, lam_init)


def _attention_ctx(lam, gain, p_c, lam_init):
    nb, t, _ = p_c.shape
    col = lambda off: (lambda b, h: (b, 0, off // V_DIM + h))
    return pl.pallas_call(
        functools.partial(_attn_ctx_kernel, lam_init=lam_init),
        grid=(nb, N_HEADS),
        in_specs=[
            pl.BlockSpec((1, LANES), lambda b, h: (0, 0)),
            pl.BlockSpec((1, V_DIM), lambda b, h: (0, 0)),
            pl.BlockSpec((1, t, V_DIM), col(Q_OFF)),
            pl.BlockSpec((1, t, V_DIM), col(G_OFF)),
            pl.BlockSpec((1, t, V_DIM), col(K_OFF)),
            pl.BlockSpec((1, t, V_DIM), col(V_OFF)),
        ],
        out_specs=pl.BlockSpec((1, t, V_DIM), col(0)),
        out_shape=jax.ShapeDtypeStruct((nb, t, D_ATTN), BF16),
        compiler_params=_params("arbitrary", "arbitrary"),
        name="diff_attn_ctx",
    )(lam, gain, p_c, p_c, p_c, p_c)


def _attn_pipe_kernel(lam_ref, gain_ref, q_ref, ag_ref, kc_ref, kl_ref, vc_ref, vl_ref, o_ref,
                      s0, s1, m0, m1, p0, p1, l0, l1, *, lam_init):
    t = pl.program_id(0)
    tq = q_ref.shape[1]
    nt = (((1,), (1,)), ((), ()))
    tn = (((0,), (0,)), ((), ()))
    chunks = ([(kc_ref, vc_ref, r) for r in range(0, kc_ref.shape[1], ATTN_CHUNK)]
              + [(kl_ref, vl_ref, r) for r in range(0, kl_ref.shape[1], ATTN_CHUNK)])
    groups = ATTN_CHUNK // 8

    @pl.when(t == 0)
    def _():
        s1[...] = jnp.zeros_like(s1)
        m1[...] = jnp.zeros_like(m1)
        p0[...] = jnp.zeros_like(p0)
        l0[...] = jnp.ones_like(l0)

    def step(s_w, m_w, s_r, m_r, p_w, l_w, p_r, l_r):
        qs = _split_halves(q_ref[0])
        m_prev = m_r[...]
        den = l_r[...]
        den1, den2 = den[:, :tq], den[:, tq:]
        ratio = (lam_ref[:, 0:1] * den1 / den2).astype(BF16)
        m_acc = jnp.full((8, 2 * tq), -jnp.inf, F32)
        l_acc = jnp.zeros((8, 2 * tq), F32)
        o_t = jnp.zeros((V_DIM, tq), F32)
        scores = {id(r): lax.dot_general(r[0], qs, nt, preferred_element_type=F32) for r in (kc_ref, kl_ref)}
        for c, (k_ref, v_ref, r0) in enumerate(chunks):
            rows = slice(c * ATTN_CHUNK, (c + 1) * ATTN_CHUNK)
            s = scores[id(k_ref)][r0:r0 + ATTN_CHUNK]
            s_w[rows, :] = s
            pace = jnp.where(l_acc < 0.0, 1.0, 0.0)
            m_acc = jnp.maximum(m_acc + pace, s.reshape(groups, 8, 2 * tq).max(axis=0))
            p = jnp.exp2(s_r[rows, :] - m_prev)
            l_acc = l_acc + p.reshape(groups, 8, 2 * tq).sum(axis=0)
            p_w[rows, :] = p.astype(BF16)
            pr = p_r[rows, :]
            a = pr[:, :tq] - ratio * pr[:, tq:]
            o_t = o_t + lax.dot_general(v_ref[0, r0:r0 + ATTN_CHUNK, :], a, tn, preferred_element_type=F32)
        m_w[...] = m_acc.max(axis=0, keepdims=True)
        l_w[...] = l_acc.sum(axis=0, keepdims=True)
        o = (o_t * (1.0 / den1)).T
        o_ref[0] = _subln_gate(o, gain_ref, ag_ref, lam_init)

    @pl.when(t % 2 == 0)
    def _():
        step(s0, m0, s1, m1, p1, l1, p0, l0)

    @pl.when(t % 2 == 1)
    def _():
        step(s1, m1, s0, m0, p0, l0, p1, l1)


def _attention_pipelined(lam, gain, p_l, p_c, kc_off, vc_off, tq, lam_init):
    nb, seq, _ = p_l.shape
    n_ctx = p_c.shape[1]
    nq = seq // tq
    n_tiles = nb * N_HEADS * nq
    depth = 2

    def tile(t):
        t = jnp.clip(t, 0, n_tiles - 1)
        return t // (N_HEADS * nq), (t // nq) % N_HEADS, t % nq

    def q_map(off):
        def f(t):
            b, h, i = tile(t)
            return b, i, off // V_DIM + h
        return f

    def kv_map(off, lag):
        def f(t):
            b, h, _ = tile(t - lag)
            return b, 0, off // V_DIM + h
        return f

    def out_map(off):
        def f(t):
            b, h, i = tile(t - depth)
            return b, i, off // V_DIM + h
        return f

    n_keys = n_ctx + seq
    return pl.pallas_call(
        functools.partial(_attn_pipe_kernel, lam_init=lam_init),
        grid=(n_tiles + depth,),
        in_specs=[
            pl.BlockSpec((1, LANES), lambda t: (0, 0)),
            pl.BlockSpec((1, V_DIM), lambda t: (0, 0)),
            pl.BlockSpec((1, tq, V_DIM), q_map(Q_OFF)),
            pl.BlockSpec((1, tq, V_DIM), out_map(G_OFF)),
            pl.BlockSpec((1, n_ctx, V_DIM), kv_map(kc_off, 0)),
            pl.BlockSpec((1, seq, V_DIM), kv_map(K_OFF, 0)),
            pl.BlockSpec((1, n_ctx, V_DIM), kv_map(vc_off, depth)),
            pl.BlockSpec((1, seq, V_DIM), kv_map(V_OFF, depth)),
        ],
        out_specs=pl.BlockSpec((1, tq, V_DIM), out_map(0)),
        out_shape=jax.ShapeDtypeStruct((nb, seq, D_ATTN), BF16),
        scratch_shapes=[
            pltpu.VMEM((n_keys, 2 * tq), F32), pltpu.VMEM((n_keys, 2 * tq), F32),
            pltpu.VMEM((1, 2 * tq), F32), pltpu.VMEM((1, 2 * tq), F32),
            pltpu.VMEM((n_keys, 2 * tq), BF16), pltpu.VMEM((n_keys, 2 * tq), BF16),
            pltpu.VMEM((1, 2 * tq), F32), pltpu.VMEM((1, 2 * tq), F32),
        ],
        compiler_params=_params("arbitrary", flags=ATTN_FLAGS),
        name="diff_attn_pipelined",
    )(lam, gain, p_l, p_l, p_c, p_l, p_c, p_l)


def _mix_kernel(pc_ref, cup_ref, ccp_ref, cun_ref, ccn_ref, pp_ref, pup_ref, pun_ref, ya_ref, x_ref, g_ref,
                cw_ref, pw_ref, ps_ref, wo_ref, lg_ref, lb_ref, o_ref, mix_ref, *, seq_len):
    i = pl.program_id(1)
    tm = x_ref.shape[1]
    n_tiles = seq_len // tm
    has_prev = i > 0
    has_next = i < n_tiles - 1
    row = lax.broadcasted_iota(jnp.int32, (tm, 1), 0)

    cu = pc_ref[0, :, 0:D_CONV].astype(F32)
    cb = pc_ref[0, :, D_CONV:2 * D_CONV].astype(F32)
    cc = pc_ref[0, :, 2 * D_CONV:3 * D_CONV].astype(F32)
    cg = pc_ref[0, :, 3 * D_CONV:4 * D_CONV].astype(F32)
    v = cc * cu
    v_before = (ccp_ref[0].astype(F32) * cup_ref[0].astype(F32))[HALO - 1:HALO]
    v_after = (ccn_ref[0].astype(F32) * cun_ref[0].astype(F32))[0:1]
    v_before = jnp.where(has_prev, v_before, 0.0)
    v_after = jnp.where(has_next, v_after, 0.0)
    v_m1 = jnp.where(row == 0, v_before, pltpu.roll(v, 1, 0))
    v_p1 = jnp.where(row == tm - 1, v_after, pltpu.roll(v, tm - 1, 0))
    cw = cw_ref[...]
    y_conv = _silu(cg) * (cb * (v_m1 * cw[0:1] + v * cw[1:2] + v_p1 * cw[2:3]))
    mix_ref[:, 0:D_CONV] = y_conv.astype(BF16)

    mix_ref[:, D_CONV:D_CONV + D_ATTN] = ya_ref[0]

    u = pp_ref[0, :, 0:D_POOL].astype(F32)
    pg = pp_ref[0, :, D_POOL:2 * D_POOL].astype(F32)
    u_before = jnp.where(has_prev, pup_ref[0].astype(F32), 0.0)
    u_after = jnp.where(has_next, pun_ref[0].astype(F32), 0.0)
    u_ext = jnp.concatenate([u_before, u, u_after], axis=0)
    n_ext = tm + 2 * HALO
    t = i * tm + row
    for gi, w in enumerate(POOL_WINDOWS):
        sl = slice(gi * POOL_GROUP, (gi + 1) * POOL_GROUP)
        s = u_ext[:, sl]
        k = 1
        while k < w:
            s = s + pltpu.roll(s, n_ext - k, 0)
            k *= 2
        win = pltpu.roll(s, w // 2, 0)[HALO:HALO + tm]
        lo = jnp.maximum(t - w // 2, 0)
        hi = jnp.minimum(t + w - w // 2, seq_len)
        d = (win / (hi - lo).astype(F32) - u[:, sl]).astype(BF16)
        yg = jnp.dot(d, pw_ref[gi], preferred_element_type=F32)
        y_pool = _silu(pg[:, sl]) * (yg * ps_ref[:, sl])
        mix_ref[:, D_CONV + D_ATTN + gi * POOL_GROUP:D_CONV + D_ATTN + (gi + 1) * POOL_GROUP] = y_pool.astype(BF16)

    out = jnp.dot(mix_ref[...], wo_ref[...], preferred_element_type=F32)
    z = DEEPNORM_ALPHA * x_ref[0] + g_ref[0] * out
    mu = jnp.mean(z, axis=-1, keepdims=True)
    zc = z - mu
    var = jnp.mean(zc * zc, axis=-1, keepdims=True)
    o_ref[0] = zc * lax.rsqrt(var + LN_EPS) * lg_ref[...] + lb_ref[...]


def _mix(p, y_attn, x, mods3, mod_row, conv_w, pool_w_bf16, pool_scale, w_out_bf16, ln_g, ln_b, tm):
    nb, t, _ = x.shape
    hb = tm // HALO
    last = t // HALO - 1
    prev = lambda i: jnp.maximum(i * hb - 1, 0)
    nxt = lambda i: jnp.minimum((i + 1) * hb, last)
    cc_blk = 2
    const2 = lambda b, i: (0, 0)
    in_specs = [
        pl.BlockSpec((1, tm, 4 * D_CONV), lambda b, i: (b, i, 0)),
        pl.BlockSpec((1, HALO, D_CONV), lambda b, i: (b, prev(i), 0)),
        pl.BlockSpec((1, HALO, D_CONV), lambda b, i: (b, prev(i), cc_blk)),
        pl.BlockSpec((1, HALO, D_CONV), lambda b, i: (b, nxt(i), 0)),
        pl.BlockSpec((1, HALO, D_CONV), lambda b, i: (b, nxt(i), cc_blk)),
        pl.BlockSpec((1, tm, 2 * D_POOL), lambda b, i: (b, i, POOL_OFF // (2 * D_POOL))),
        pl.BlockSpec((1, HALO, D_POOL), lambda b, i: (b, prev(i), POOL_OFF // D_POOL)),
        pl.BlockSpec((1, HALO, D_POOL), lambda b, i: (b, nxt(i), POOL_OFF // D_POOL)),
        pl.BlockSpec((1, tm, D_ATTN), lambda b, i: (b, i, 0)),
        pl.BlockSpec((1, tm, D_MODEL), lambda b, i: (b, i, 0)),
        pl.BlockSpec((1, 1, D_MODEL), lambda b, i: (mod_row(b), 0, 2)),
        pl.BlockSpec(conv_w.shape, const2),
        pl.BlockSpec(pool_w_bf16.shape, lambda b, i: (0, 0, 0)),
        pl.BlockSpec(pool_scale.shape, const2),
        pl.BlockSpec(w_out_bf16.shape, const2, pipeline_mode=pl.Buffered(1)),
        pl.BlockSpec(ln_g.shape, const2),
        pl.BlockSpec(ln_b.shape, const2),
    ]
    return pl.pallas_call(
        functools.partial(_mix_kernel, seq_len=t),
        grid=(nb, t // tm),
        in_specs=in_specs,
        out_specs=pl.BlockSpec((1, tm, D_MODEL), lambda b, i: (b, i, 0)),
        out_shape=jax.ShapeDtypeStruct((nb, t, D_MODEL), F32),
        scratch_shapes=[pltpu.VMEM((tm, D_MODEL), BF16)],
        compiler_params=_params("arbitrary", "arbitrary"),
        name="mix_out_ln",
    )(p, p, p, p, p, p, p, p, y_attn, x, mods3, conv_w, pool_w_bf16, pool_scale, w_out_bf16, ln_g, ln_b)


def _rope_tables(seq_len):
    n = QK_DIM // 4
    inv = ROPE_THETA ** (-np.arange(n, dtype=np.float64) / n)
    tok = np.arange(seq_len)
    pos = np.stack([tok // GRID_W, tok % GRID_W], axis=1).astype(np.float64)
    m = np.arange(LANES)
    ang = pos[:, (m % QK_DIM) // (QK_DIM // 2)] * inv[m % n][None, :]
    second = ((m % (QK_DIM // 2)) >= n)[None, :]
    cos, sin = np.cos(ang), np.sin(ang)
    sa = np.where(second, 0.0, -sin)
    sb = np.where(second, sin, 0.0)
    return tuple(jnp.asarray(a, dtype=F32) for a in (cos, sa, sb))


def kernel(x, c, ctx, c_ctx, w_mod, b_mod, w_in, conv_w, lam_q1, lam_k1, lam_q2, lam_k2, subln_g, pool_w,
           pool_scale, w_out, ln_g, ln_b):
    nb, seq, _ = x.shape
    ctx_len = ctx.shape[1]
    assert nb + 1 <= MOD_ROWS

    cvec = jnp.concatenate([c, c_ctx[None, :], jnp.zeros((MOD_ROWS - nb - 1, D_MODEL), F32)], axis=0)
    lamv = jnp.stack([lam_q1, lam_k1, lam_q2, lam_k2], axis=1).astype(F32)
    mods, lams = _modulation(cvec, w_mod, b_mod, lamv)
    tables = _rope_tables(seq)
    latent_row = lambda b: b
    ctx_row = lambda b: nb

    xl, xc = x, ctx
    for l in range(DEPTH):
        last = l == DEPTH - 1
        lam_init = 0.8 - 0.6 * math.exp(-0.3 * l)
        mods3 = mods[l].reshape(MOD_ROWS, 1, 3 * D_MODEL)
        w_in_l = w_in[l].astype(BF16)
        w_out_l = w_out[l].astype(BF16)
        pool_w_l = pool_w[l].astype(BF16)
        gain = subln_g[l].reshape(1, V_DIM)
        p_l = _project(xl, mods3, latent_row, w_in_l, 0, D_IN // PROJ_TN, PROJ_TM, tables)
        xc_flat = xc.reshape(1, nb * ctx_len, D_MODEL)
        if last:
            p_c = _project(xc_flat, mods3, ctx_row, w_in_l, K_OFF // PROJ_TN, 2 * D_ATTN // PROJ_TN,
                           nb * ctx_len, None).reshape(nb, ctx_len, 2 * D_ATTN)
            kc_off, vc_off = 0, D_ATTN
        else:
            p_c = _project(xc_flat, mods3, ctx_row, w_in_l, 0, D_IN // PROJ_TN, nb * ctx_len,
                           None).reshape(nb, ctx_len, D_IN)
            kc_off, vc_off = K_OFF, V_OFF
        y_attn = _attention_pipelined(lams[l], gain, p_l, p_c, kc_off, vc_off, ATTN_TQ, lam_init)
        mix_args = (conv_w[l], pool_w_l, pool_scale[l].reshape(1, D_POOL), w_out_l,
                    ln_g[l].reshape(1, D_MODEL), ln_b[l].reshape(1, D_MODEL))
        xl_new = _mix(p_l, y_attn, xl, mods3, latent_row, *mix_args, MIX_TM)
        if not last:
            y_attn_c = _attention_ctx(lams[l], gain, p_c, lam_init)
            xc = _mix(p_c, y_attn_c, xc, mods3, ctx_row, *mix_args, ctx_len)
        xl = xl_new
    return xl
```

```python
import functools
import math

import numpy as np
import jax
import jax.numpy as jnp
from jax import lax
from jax.experimental import pallas as pl
from jax.experimental.pallas import tpu as pltpu

F32 = jnp.float32
BF16 = jnp.bfloat16

D_MODEL = 2048
DEPTH = 2
GRID_W = 64
D_CONV = 512
D_ATTN = 1024
D_POOL = 512
N_HEADS = 8
V_DIM = 128
QK_DIM = 64
ATTN_SCALE = QK_DIM ** -0.5
LOG2_E = math.log2(math.e)
Q_PRESCALE = ATTN_SCALE * LOG2_E
POOL_WINDOWS = (2, 4, 8, 16)
POOL_GROUP = 128
ROPE_THETA = 10000.0
LN_EPS = 1e-5
RMS_EPS = 1e-5
DEEPNORM_ALPHA = (2 * DEPTH) ** 0.25
D_IN = 4 * D_CONV + 4 * D_ATTN + 2 * D_POOL
Q_OFF = 4 * D_CONV
K_OFF = Q_OFF + D_ATTN
V_OFF = K_OFF + D_ATTN
G_OFF = V_OFF + D_ATTN
POOL_OFF = G_OFF + D_ATTN

LANES = 128
SUBLANES = 8
BF16_ROWS = 16
HALO = BF16_ROWS
MOD_ROWS = 8
VMEM_LIMIT = 56 * 1024 * 1024

PROJ_TM = 1024
PROJ_TN = 1024
ATTN_TQ = 256
ATTN_CHUNK = 256
MIX_TM = 256
MOD_TN = 1536


def _silu(x):
    return x / (1.0 + jnp.exp(-x))


def _params(*sem):
    return pltpu.CompilerParams(dimension_semantics=sem, vmem_limit_bytes=VMEM_LIMIT)


def _mod_kernel(c_ref, w_ref, b_ref, lamv_ref, mod_ref, lam_ref):
    layer = pl.program_id(0)
    s = _silu(c_ref[...]).astype(BF16)
    w = w_ref[0].astype(BF16)
    mod_ref[0] = jnp.dot(s, w, preferred_element_type=F32) + b_ref[0]
    lv = lamv_ref[0]
    d1 = jnp.sum(lv[0:1] * lv[1:2], axis=-1, keepdims=True)
    d2 = jnp.sum(lv[2:3] * lv[3:4], axis=-1, keepdims=True)
    lf = jnp.zeros((1, LANES), F32) + layer.astype(F32)
    lam_init = 0.8 - 0.6 * jnp.exp(-0.3 * lf)
    lam_ref[0] = jnp.exp(d1) - jnp.exp(d2) + lam_init


def _modulation(cvec, w_mod, b_mod, lamv):
    n_out = w_mod.shape[-1]
    return pl.pallas_call(
        _mod_kernel,
        grid=(DEPTH, n_out // MOD_TN),
        in_specs=[
            pl.BlockSpec((MOD_ROWS, D_MODEL), lambda l, j: (0, 0)),
            pl.BlockSpec((1, D_MODEL, MOD_TN), lambda l, j: (l, 0, j)),
            pl.BlockSpec((1, 1, MOD_TN), lambda l, j: (l, 0, j)),
            pl.BlockSpec((1, 4, QK_DIM), lambda l, j: (l, 0, 0)),
        ],
        out_specs=[
            pl.BlockSpec((1, MOD_ROWS, MOD_TN), lambda l, j: (l, 0, j)),
            pl.BlockSpec((1, 1, LANES), lambda l, j: (l, 0, 0)),
        ],
        out_shape=[
            jax.ShapeDtypeStruct((DEPTH, MOD_ROWS, n_out), F32),
            jax.ShapeDtypeStruct((DEPTH, 1, LANES), F32),
        ],
        compiler_params=_params("arbitrary", "arbitrary"),
        name="modulation",
    )(cvec, w_mod, b_mod.reshape(DEPTH, 1, n_out), lamv)


def _proj_kernel(*refs, rope, q_blk, k_blk):
    if rope:
        x_ref, sh_ref, sc_ref, w_ref, cos_ref, sin_ref, o_ref, h_ref = refs
    else:
        x_ref, sh_ref, sc_ref, w_ref, o_ref, h_ref = refs
    j = pl.program_id(2)

    @pl.when(j == 0)
    def _():
        h_ref[...] = (x_ref[0] * (1.0 + sc_ref[0]) + sh_ref[0]).astype(BF16)

    acc = jnp.dot(h_ref[...], w_ref[...], preferred_element_type=F32)
    o_ref[0] = acc.astype(BF16)

    if rope:
        @pl.when((j == q_blk) | (j == k_blk))
        def _():
            scale = jnp.where(j == q_blk, Q_PRESCALE, 1.0).astype(F32)
            cos = cos_ref[...] * scale
            sin = sin_ref[...] * scale
            for c in range(acc.shape[1] // LANES):
                xc = acc[:, c * LANES:(c + 1) * LANES]
                r = xc * cos + pltpu.roll(xc, LANES // 2, 1) * sin
                o_ref[0, :, c * LANES:(c + 1) * LANES] = r.astype(BF16)


def _project(x, mods3, mod_row, w_bf16, col_blk0, n_col_blks, tm, rope_tables):
    nb, t, _ = x.shape
    tn = PROJ_TN
    rope = rope_tables is not None
    in_specs = [
        pl.BlockSpec((1, tm, D_MODEL), lambda b, i, j: (b, i, 0)),
        pl.BlockSpec((1, 1, D_MODEL), lambda b, i, j: (mod_row(b), 0, 0)),
        pl.BlockSpec((1, 1, D_MODEL), lambda b, i, j: (mod_row(b), 0, 1)),
        pl.BlockSpec((D_MODEL, tn), lambda b, i, j: (0, j + col_blk0)),
    ]
    args = [x, mods3, mods3, w_bf16]
    if rope:
        in_specs += [pl.BlockSpec((tm, LANES), lambda b, i, j: (i, 0))] * 2
        args += list(rope_tables)
    return pl.pallas_call(
        functools.partial(_proj_kernel, rope=rope, q_blk=Q_OFF // tn - col_blk0, k_blk=K_OFF // tn - col_blk0),
        grid=(nb, t // tm, n_col_blks),
        in_specs=in_specs,
        out_specs=pl.BlockSpec((1, tm, tn), lambda b, i, j: (b, i, j)),
        out_shape=jax.ShapeDtypeStruct((nb, t, n_col_blks * tn), BF16),
        scratch_shapes=[pltpu.VMEM((tm, D_MODEL), BF16)],
        compiler_params=_params("arbitrary", "arbitrary", "arbitrary"),
        name="proj_rope" if rope else "proj",
    )(*args)


def _head_layout(w):
    d = w.shape[0]
    n_freq = QK_DIM // 4
    qk = w[:, Q_OFF:V_OFF].reshape(d, 2, N_HEADS, 2, 2, 2, n_freq)
    qk = qk.transpose(0, 1, 2, 5, 3, 4, 6).reshape(d, 2 * D_ATTN)
    return jnp.concatenate([w[:, :Q_OFF], qk, w[:, V_OFF:]], axis=1)


def _split_halves(q):
    lane = lax.broadcasted_iota(jnp.int32, (1, LANES), 1)
    first = (lane // (QK_DIM // 2)) % 2 == 0
    zero = jnp.zeros_like(q)
    return jnp.concatenate([jnp.where(first, q, zero), jnp.where(first, zero, q)], axis=0)


def _subln_gate(o, gain_ref, ag_ref, lam_init):
    y = o * lax.rsqrt(jnp.mean(o * o, axis=-1, keepdims=True) + RMS_EPS) * (gain_ref[...] * (1.0 - lam_init))
    return (_silu(ag_ref[0].astype(F32)) * y).astype(BF16)


def _attn_ctx_kernel(lam_ref, gain_ref, q_ref, ag_ref, k_ref, v_ref, o_ref, *, lam_init):
    tq = q_ref.shape[1]
    qs = _split_halves(q_ref[0] * ATTN_SCALE)
    s = lax.dot_general(qs, k_ref[0], (((1,), (1,)), ((), ())), preferred_element_type=F32)
    p = jnp.exp(s - s.max(axis=-1, keepdims=True))
    den = p.sum(axis=-1, keepdims=True)
    den1, den2 = den[:tq], den[tq:]
    ratio = lam_ref[:, 0:1] * den1 / den2
    a = (p[:tq] - ratio * p[tq:]).astype(BF16)
    o = jnp.dot(a, v_ref[0], preferred_element_type=F32) / den1
    o_ref[0] = _subln_gate(o, gain_ref, ag_ref, lam_init)


def _attention_ctx(lam, gain, p_c, lam_init):
    nb, t, _ = p_c.shape
    col = lambda off: (lambda b, h: (b, 0, off // V_DIM + h))
    return pl.pallas_call(
        functools.partial(_attn_ctx_kernel, lam_init=lam_init),
        grid=(nb, N_HEADS),
        in_specs=[
            pl.BlockSpec((1, LANES), lambda b, h: (0, 0)),
            pl.BlockSpec((1, V_DIM), lambda b, h: (0, 0)),
            pl.BlockSpec((1, t, V_DIM), col(Q_OFF)),
            pl.BlockSpec((1, t, V_DIM), col(G_OFF)),
            pl.BlockSpec((1, t, V_DIM), col(K_OFF)),
            pl.BlockSpec((1, t, V_DIM), col(V_OFF)),
        ],
        out_specs=pl.BlockSpec((1, t, V_DIM), col(0)),
        out_shape=jax.ShapeDtypeStruct((nb, t, D_ATTN), BF16),
        compiler_params=_params("arbitrary", "arbitrary"),
        name="diff_attn_ctx",
    )(lam, gain, p_c, p_c, p_c, p_c)


def _attn_kernel(lam_ref, gain_ref, q_ref, ag_ref, kc_ref, kl_ref, vc_ref, vl_ref, o_ref,
                 s0, s1, m0, m1, p0, p1, l0, l1, *, lam_init):
    step = pl.program_id(0)
    tq = q_ref.shape[1] // 2
    nt = (((1,), (1,)), ((), ()))
    tn = (((0,), (0,)), ((), ()))
    chunks = ([(kc_ref, vc_ref, r) for r in range(0, kc_ref.shape[1], ATTN_CHUNK)]
              + [(kl_ref, vl_ref, r) for r in range(0, kl_ref.shape[1], ATTN_CHUNK)])
    groups = ATTN_CHUNK // SUBLANES

    @pl.when(step == 0)
    def _():
        s1[...] = jnp.zeros_like(s1)
        m1[...] = jnp.zeros_like(m1)
        p0[...] = jnp.zeros_like(p0)
        l0[...] = jnp.ones_like(l0)

    def half(tile_rows, s_w, m_w, s_r, m_r, p_w, l_w, p_r, l_r):
        qs = _split_halves(q_ref[0, tile_rows, :])
        m_prev = m_r[...]
        den = l_r[...]
        den1, den2 = den[:, :tq], den[:, tq:]
        ratio = (lam_ref[:, 0:1] * den1 / den2).astype(BF16)
        m_acc = jnp.full((SUBLANES, 2 * tq), -jnp.inf, F32)
        l_acc = jnp.zeros((SUBLANES, 2 * tq), F32)
        o_t = jnp.zeros((V_DIM, tq), F32)
        scores = {id(r): lax.dot_general(r[0], qs, nt, preferred_element_type=F32) for r in (kc_ref, kl_ref)}
        for c, (k_ref, v_ref, r0) in enumerate(chunks):
            rows = slice(c * ATTN_CHUNK, (c + 1) * ATTN_CHUNK)
            s = scores[id(k_ref)][r0:r0 + ATTN_CHUNK]
            s_w[rows, :] = s
            pace = jnp.where(l_acc < 0.0, 1.0, 0.0)
            m_acc = jnp.maximum(m_acc + pace, s.reshape(groups, SUBLANES, 2 * tq).max(axis=0))
            p = jnp.exp2(s_r[rows, :] - m_prev)
            l_acc = l_acc + p.reshape(groups, SUBLANES, 2 * tq).sum(axis=0)
            p_w[rows, :] = p.astype(BF16)
            pr = p_r[rows, :]
            a = pr[:, :tq] - ratio * pr[:, tq:]
            o_t = o_t + lax.dot_general(v_ref[0, r0:r0 + ATTN_CHUNK, :], a, tn, preferred_element_type=F32)
        m_w[...] = m_acc.max(axis=0, keepdims=True)
        l_w[...] = l_acc.sum(axis=0, keepdims=True)
        o = (o_t * (1.0 / den1)).T
        y = o * lax.rsqrt(jnp.mean(o * o, axis=-1, keepdims=True) + RMS_EPS) * (gain_ref[...] * (1.0 - lam_init))
        o_ref[0, tile_rows, :] = (_silu(ag_ref[0, tile_rows, :].astype(F32)) * y).astype(BF16)

    half(slice(0, tq), s0, m0, s1, m1, p1, l1, p0, l0)
    half(slice(tq, 2 * tq), s1, m1, s0, m0, p0, l0, p1, l1)


def _attention(lam, gain, p_l, p_c, kc_off, vc_off, tq, lam_init):
    nb, seq, _ = p_l.shape
    n_ctx = p_c.shape[1]
    assert n_ctx % ATTN_CHUNK == 0 and seq % ATTN_CHUNK == 0 and seq % (2 * tq) == 0
    n_keys = n_ctx + seq
    pairs_per_head = seq // (2 * tq)
    n_pairs = nb * N_HEADS * pairs_per_head

    def pair(k):
        k = jnp.clip(k, 0, n_pairs - 1)
        return k // (N_HEADS * pairs_per_head), (k // pairs_per_head) % N_HEADS, k % pairs_per_head

    def tile_map(off, lag):
        def f(k):
            b, h, i = pair(k - lag)
            return b, i, off // V_DIM + h
        return f

    def head_map(off, lag):
        def f(k):
            b, h, _ = pair(k - lag)
            return b, 0, off // V_DIM + h
        return f

    return pl.pallas_call(
        functools.partial(_attn_kernel, lam_init=lam_init),
        grid=(n_pairs + 1,),
        in_specs=[
            pl.BlockSpec((1, LANES), lambda k: (0, 0)),
            pl.BlockSpec((1, V_DIM), lambda k: (0, 0)),
            pl.BlockSpec((1, 2 * tq, V_DIM), tile_map(Q_OFF, 0)),
            pl.BlockSpec((1, 2 * tq, V_DIM), tile_map(G_OFF, 1)),
            pl.BlockSpec((1, n_ctx, V_DIM), head_map(kc_off, 0)),
            pl.BlockSpec((1, seq, V_DIM), head_map(K_OFF, 0)),
            pl.BlockSpec((1, n_ctx, V_DIM), head_map(vc_off, 1)),
            pl.BlockSpec((1, seq, V_DIM), head_map(V_OFF, 1)),
        ],
        out_specs=pl.BlockSpec((1, 2 * tq, V_DIM), tile_map(0, 1)),
        out_shape=jax.ShapeDtypeStruct((nb, seq, D_ATTN), BF16),
        scratch_shapes=[
            pltpu.VMEM((n_keys, 2 * tq), F32), pltpu.VMEM((n_keys, 2 * tq), F32),
            pltpu.VMEM((1, 2 * tq), F32), pltpu.VMEM((1, 2 * tq), F32),
            pltpu.VMEM((n_keys, 2 * tq), BF16), pltpu.VMEM((n_keys, 2 * tq), BF16),
            pltpu.VMEM((1, 2 * tq), F32), pltpu.VMEM((1, 2 * tq), F32),
        ],
        compiler_params=_params("arbitrary"),
        name="diff_attn",
    )(lam, gain, p_l, p_l, p_c, p_l, p_c, p_l)


def _mix_kernel(pc_ref, cup_ref, ccp_ref, cun_ref, ccn_ref, pp_ref, pup_ref, pun_ref, ya_ref, x_ref, g_ref,
                cw_ref, pw_ref, ps_ref, wo_ref, lg_ref, lb_ref, o_ref, mix_ref, *, seq_len):
    i = pl.program_id(1)
    tm = x_ref.shape[1]
    n_tiles = seq_len // tm
    has_prev = i > 0
    has_next = i < n_tiles - 1
    row = lax.broadcasted_iota(jnp.int32, (tm, 1), 0)

    cu = pc_ref[0, :, 0:D_CONV].astype(F32)
    cb = pc_ref[0, :, D_CONV:2 * D_CONV].astype(F32)
    cc = pc_ref[0, :, 2 * D_CONV:3 * D_CONV].astype(F32)
    cg = pc_ref[0, :, 3 * D_CONV:4 * D_CONV].astype(F32)
    v = cc * cu
    v_before = (ccp_ref[0].astype(F32) * cup_ref[0].astype(F32))[HALO - 1:HALO]
    v_after = (ccn_ref[0].astype(F32) * cun_ref[0].astype(F32))[0:1]
    v_before = jnp.where(has_prev, v_before, 0.0)
    v_after = jnp.where(has_next, v_after, 0.0)
    v_m1 = jnp.where(row == 0, v_before, pltpu.roll(v, 1, 0))
    v_p1 = jnp.where(row == tm - 1, v_after, pltpu.roll(v, tm - 1, 0))
    cw = cw_ref[...]
    y_conv = _silu(cg) * (cb * (v_m1 * cw[0:1] + v * cw[1:2] + v_p1 * cw[2:3]))
    mix_ref[:, 0:D_CONV] = y_conv.astype(BF16)

    mix_ref[:, D_CONV:D_CONV + D_ATTN] = ya_ref[0]

    u = pp_ref[0, :, 0:D_POOL].astype(F32)
    pg = pp_ref[0, :, D_POOL:2 * D_POOL].astype(F32)
    u_before = jnp.where(has_prev, pup_ref[0].astype(F32), 0.0)
    u_after = jnp.where(has_next, pun_ref[0].astype(F32), 0.0)
    u_ext = jnp.concatenate([u_before, u, u_after], axis=0)
    n_ext = tm + 2 * HALO
    t = i * tm + row
    for gi, w in enumerate(POOL_WINDOWS):
        sl = slice(gi * POOL_GROUP, (gi + 1) * POOL_GROUP)
        s = u_ext[:, sl]
        k = 1
        while k < w:
            s = s + pltpu.roll(s, n_ext - k, 0)
            k *= 2
        win = pltpu.roll(s, w // 2, 0)[HALO:HALO + tm]
        lo = jnp.maximum(t - w // 2, 0)
        hi = jnp.minimum(t + w - w // 2, seq_len)
        d = (win / (hi - lo).astype(F32) - u[:, sl]).astype(BF16)
        yg = jnp.dot(d, pw_ref[gi], preferred_element_type=F32)
        y_pool = _silu(pg[:, sl]) * (yg * ps_ref[:, sl])
        mix_ref[:, D_CONV + D_ATTN + gi * POOL_GROUP:D_CONV + D_ATTN + (gi + 1) * POOL_GROUP] = y_pool.astype(BF16)

    out = jnp.dot(mix_ref[...], wo_ref[...], preferred_element_type=F32)
    z = DEEPNORM_ALPHA * x_ref[0] + g_ref[0] * out
    mu = jnp.mean(z, axis=-1, keepdims=True)
    zc = z - mu
    var = jnp.mean(zc * zc, axis=-1, keepdims=True)
    o_ref[0] = zc * lax.rsqrt(var + LN_EPS) * lg_ref[...] + lb_ref[...]


def _mix(p, y_attn, x, mods3, mod_row, conv_w, pool_w_bf16, pool_scale, w_out_bf16, ln_g, ln_b, tm):
    nb, t, _ = x.shape
    hb = tm // HALO
    last = t // HALO - 1
    prev = lambda i: jnp.maximum(i * hb - 1, 0)
    nxt = lambda i: jnp.minimum((i + 1) * hb, last)
    cc_blk = 2
    const2 = lambda b, i: (0, 0)
    in_specs = [
        pl.BlockSpec((1, tm, 4 * D_CONV), lambda b, i: (b, i, 0)),
        pl.BlockSpec((1, HALO, D_CONV), lambda b, i: (b, prev(i), 0)),
        pl.BlockSpec((1, HALO, D_CONV), lambda b, i: (b, prev(i), cc_blk)),
        pl.BlockSpec((1, HALO, D_CONV), lambda b, i: (b, nxt(i), 0)),
        pl.BlockSpec((1, HALO, D_CONV), lambda b, i: (b, nxt(i), cc_blk)),
        pl.BlockSpec((1, tm, 2 * D_POOL), lambda b, i: (b, i, POOL_OFF // (2 * D_POOL))),
        pl.BlockSpec((1, HALO, D_POOL), lambda b, i: (b, prev(i), POOL_OFF // D_POOL)),
        pl.BlockSpec((1, HALO, D_POOL), lambda b, i: (b, nxt(i), POOL_OFF // D_POOL)),
        pl.BlockSpec((1, tm, D_ATTN), lambda b, i: (b, i, 0)),
        pl.BlockSpec((1, tm, D_MODEL), lambda b, i: (b, i, 0)),
        pl.BlockSpec((1, 1, D_MODEL), lambda b, i: (mod_row(b), 0, 2)),
        pl.BlockSpec(conv_w.shape, const2),
        pl.BlockSpec(pool_w_bf16.shape, lambda b, i: (0, 0, 0)),
        pl.BlockSpec(pool_scale.shape, const2),
        pl.BlockSpec(w_out_bf16.shape, const2, pipeline_mode=pl.Buffered(1)),
        pl.BlockSpec(ln_g.shape, const2),
        pl.BlockSpec(ln_b.shape, const2),
    ]
    return pl.pallas_call(
        functools.partial(_mix_kernel, seq_len=t),
        grid=(nb, t // tm),
        in_specs=in_specs,
        out_specs=pl.BlockSpec((1, tm, D_MODEL), lambda b, i: (b, i, 0)),
        out_shape=jax.ShapeDtypeStruct((nb, t, D_MODEL), F32),
        scratch_shapes=[pltpu.VMEM((tm, D_MODEL), BF16)],
        compiler_params=_params("arbitrary", "arbitrary"),
        name="mix_out_ln",
    )(p, p, p, p, p, p, p, p, y_attn, x, mods3, conv_w, pool_w_bf16, pool_scale, w_out_bf16, ln_g, ln_b)


def _rope_tables(seq_len):
    n = QK_DIM // 4
    inv = ROPE_THETA ** (-np.arange(n, dtype=np.float64) / n)
    tok = np.arange(seq_len)
    pos = np.stack([tok // GRID_W, tok % GRID_W], axis=1).astype(np.float64)
    m = np.arange(LANES)
    ang = pos[:, (m // n) % 2] * inv[m % n][None, :]
    second = (m >= LANES // 2)[None, :]
    cos, sin = np.cos(ang), np.sin(ang)
    return jnp.asarray(cos, dtype=F32), jnp.asarray(np.where(second, sin, -sin), dtype=F32)


def kernel(x, c, ctx, c_ctx, w_mod, b_mod, w_in, conv_w, lam_q1, lam_k1, lam_q2, lam_k2, subln_g, pool_w,
           pool_scale, w_out, ln_g, ln_b):
    nb, seq, _ = x.shape
    ctx_len = ctx.shape[1]
    assert nb + 1 <= MOD_ROWS

    cvec = jnp.concatenate([c, c_ctx[None, :], jnp.zeros((MOD_ROWS - nb - 1, D_MODEL), F32)], axis=0)
    lamv = jnp.stack([lam_q1, lam_k1, lam_q2, lam_k2], axis=1).astype(F32)
    mods, lams = _modulation(cvec, w_mod, b_mod, lamv)
    tables = _rope_tables(seq)
    latent_row = lambda b: b
    ctx_row = lambda b: nb

    xl, xc = x, ctx
    for l in range(DEPTH):
        last = l == DEPTH - 1
        lam_init = 0.8 - 0.6 * math.exp(-0.3 * l)
        mods3 = mods[l].reshape(MOD_ROWS, 1, 3 * D_MODEL)
        w_in_l = _head_layout(w_in[l]).astype(BF16)
        w_out_l = w_out[l].astype(BF16)
        pool_w_l = pool_w[l].astype(BF16)
        gain = subln_g[l].reshape(1, V_DIM)
        p_l = _project(xl, mods3, latent_row, w_in_l, 0, D_IN // PROJ_TN, PROJ_TM, tables)
        xc_flat = xc.reshape(1, nb * ctx_len, D_MODEL)
        if last:
            p_c = _project(xc_flat, mods3, ctx_row, w_in_l, K_OFF // PROJ_TN, 2 * D_ATTN // PROJ_TN,
                           nb * ctx_len, None).reshape(nb, ctx_len, 2 * D_ATTN)
            kc_off, vc_off = 0, D_ATTN
        else:
            p_c = _project(xc_flat, mods3, ctx_row, w_in_l, 0, D_IN // PROJ_TN, nb * ctx_len,
                           None).reshape(nb, ctx_len, D_IN)
            kc_off, vc_off = K_OFF, V_OFF
        y_attn = _attention(lams[l], gain, p_l, p_c, kc_off, vc_off, ATTN_TQ, lam_init)
        mix_args = (conv_w[l], pool_w_l, pool_scale[l].reshape(1, D_POOL), w_out_l,
                    ln_g[l].reshape(1, D_MODEL), ln_b[l].reshape(1, D_MODEL))
        xl_new = _mix(p_l, y_attn, xl, mods3, latent_row, *mix_args, MIX_TM)
        if not last:
            y_attn_c = _attention_ctx(lams[l], gain, p_c, lam_init)
            xc = _mix(p_c, y_attn_c, xc, mods3, ctx_row, *mix_args, ctx_len)
        xl = xl_new
    return xl
```

```python
import functools
import math

import numpy as np
import jax
import jax.numpy as jnp
from jax import lax
from jax.experimental import pallas as pl
from jax.experimental.pallas import tpu as pltpu

F32 = jnp.float32
BF16 = jnp.bfloat16

D_MODEL = 2048
DEPTH = 2
GRID_W = 64
D_CONV = 512
D_ATTN = 1024
D_POOL = 512
N_HEADS = 8
V_DIM = 128
QK_DIM = 64
ATTN_SCALE = QK_DIM ** -0.5
LOG2_E = math.log2(math.e)
Q_PRESCALE = ATTN_SCALE * LOG2_E
POOL_WINDOWS = (2, 4, 8, 16)
POOL_GROUP = 128
ROPE_THETA = 10000.0
LN_EPS = 1e-5
RMS_EPS = 1e-5
DEEPNORM_ALPHA = (2 * DEPTH) ** 0.25
D_IN = 4 * D_CONV + 4 * D_ATTN + 2 * D_POOL
Q_OFF = 4 * D_CONV
K_OFF = Q_OFF + D_ATTN
V_OFF = K_OFF + D_ATTN
G_OFF = V_OFF + D_ATTN
POOL_OFF = G_OFF + D_ATTN

LANES = 128
SUBLANES = 8
BF16_ROWS = 16
HALO = BF16_ROWS
MOD_ROWS = 8
VMEM_LIMIT = 56 * 1024 * 1024

PROJ_TM = 1024
PROJ_TN = 1024
ATTN_TQ = 256
ATTN_CHUNK = 256
MIX_TM = 512
MIX_SUB = 256
MOD_TN = 1536


def _silu(x):
    return x / (1.0 + jnp.exp(-x))


def _params(*sem):
    return pltpu.CompilerParams(dimension_semantics=sem, vmem_limit_bytes=VMEM_LIMIT)


def _mod_kernel(c_ref, w_ref, b_ref, lamv_ref, mod_ref, lam_ref):
    layer = pl.program_id(0)
    s = _silu(c_ref[...]).astype(BF16)
    w = w_ref[0].astype(BF16)
    mod_ref[0] = jnp.dot(s, w, preferred_element_type=F32) + b_ref[0]
    lv = lamv_ref[0]
    d1 = jnp.sum(lv[0:1] * lv[1:2], axis=-1, keepdims=True)
    d2 = jnp.sum(lv[2:3] * lv[3:4], axis=-1, keepdims=True)
    lf = jnp.zeros((1, LANES), F32) + layer.astype(F32)
    lam_init = 0.8 - 0.6 * jnp.exp(-0.3 * lf)
    lam_ref[0] = jnp.exp(d1) - jnp.exp(d2) + lam_init


def _modulation(cvec, w_mod, b_mod, lamv):
    n_out = w_mod.shape[-1]
    return pl.pallas_call(
        _mod_kernel,
        grid=(DEPTH, n_out // MOD_TN),
        in_specs=[
            pl.BlockSpec((MOD_ROWS, D_MODEL), lambda l, j: (0, 0)),
            pl.BlockSpec((1, D_MODEL, MOD_TN), lambda l, j: (l, 0, j)),
            pl.BlockSpec((1, 1, MOD_TN), lambda l, j: (l, 0, j)),
            pl.BlockSpec((1, 4, QK_DIM), lambda l, j: (l, 0, 0)),
        ],
        out_specs=[
            pl.BlockSpec((1, MOD_ROWS, MOD_TN), lambda l, j: (l, 0, j)),
            pl.BlockSpec((1, 1, LANES), lambda l, j: (l, 0, 0)),
        ],
        out_shape=[
            jax.ShapeDtypeStruct((DEPTH, MOD_ROWS, n_out), F32),
            jax.ShapeDtypeStruct((DEPTH, 1, LANES), F32),
        ],
        compiler_params=_params("arbitrary", "arbitrary"),
        name="modulation",
    )(cvec, w_mod, b_mod.reshape(DEPTH, 1, n_out), lamv)


def _proj_kernel(*refs, rope, q_blk, k_blk):
    if rope:
        x_ref, sh_ref, sc_ref, w_ref, cos_ref, sin_ref, o_ref, h_ref = refs
    else:
        x_ref, sh_ref, sc_ref, w_ref, o_ref, h_ref = refs
    j = pl.program_id(2)

    @pl.when(j == 0)
    def _():
        h_ref[...] = (x_ref[0] * (1.0 + sc_ref[0]) + sh_ref[0]).astype(BF16)

    acc = jnp.dot(h_ref[...], w_ref[...], preferred_element_type=F32)
    o_ref[0] = acc.astype(BF16)

    if rope:
        @pl.when((j == q_blk) | (j == k_blk))
        def _():
            scale = jnp.where(j == q_blk, Q_PRESCALE, 1.0).astype(F32)
            cos = cos_ref[...] * scale
            sin = sin_ref[...] * scale
            for c in range(acc.shape[1] // LANES):
                xc = acc[:, c * LANES:(c + 1) * LANES]
                r = xc * cos + pltpu.roll(xc, LANES // 2, 1) * sin
                o_ref[0, :, c * LANES:(c + 1) * LANES] = r.astype(BF16)


def _project(x, mods3, mod_row, w_bf16, col_blk0, n_col_blks, tm, rope_tables):
    nb, t, _ = x.shape
    tn = PROJ_TN
    rope = rope_tables is not None
    in_specs = [
        pl.BlockSpec((1, tm, D_MODEL), lambda b, i, j: (b, i, 0)),
        pl.BlockSpec((1, 1, D_MODEL), lambda b, i, j: (mod_row(b), 0, 0)),
        pl.BlockSpec((1, 1, D_MODEL), lambda b, i, j: (mod_row(b), 0, 1)),
        pl.BlockSpec((D_MODEL, tn), lambda b, i, j: (0, j + col_blk0)),
    ]
    args = [x, mods3, mods3, w_bf16]
    if rope:
        in_specs += [pl.BlockSpec((tm, LANES), lambda b, i, j: (i, 0))] * 2
        args += list(rope_tables)
    return pl.pallas_call(
        functools.partial(_proj_kernel, rope=rope, q_blk=Q_OFF // tn - col_blk0, k_blk=K_OFF // tn - col_blk0),
        grid=(nb, t // tm, n_col_blks),
        in_specs=in_specs,
        out_specs=pl.BlockSpec((1, tm, tn), lambda b, i, j: (b, i, j)),
        out_shape=jax.ShapeDtypeStruct((nb, t, n_col_blks * tn), BF16),
        scratch_shapes=[pltpu.VMEM((tm, D_MODEL), BF16)],
        compiler_params=_params("arbitrary", "arbitrary", "arbitrary"),
        name="proj_rope" if rope else "proj",
    )(*args)


def _head_layout(w):
    d = w.shape[0]
    n_freq = QK_DIM // 4
    qk = w[:, Q_OFF:V_OFF].reshape(d, 2, N_HEADS, 2, 2, 2, n_freq)
    qk = qk.transpose(0, 1, 2, 5, 3, 4, 6).reshape(d, 2 * D_ATTN)
    return lax.dynamic_update_slice(w, qk, (0, Q_OFF))


def _split_halves(q):
    lane = lax.broadcasted_iota(jnp.int32, (1, LANES), 1)
    first = (lane // (QK_DIM // 2)) % 2 == 0
    zero = jnp.zeros_like(q)
    return jnp.concatenate([jnp.where(first, q, zero), jnp.where(first, zero, q)], axis=0)


def _attn_ctx_kernel(lam_ref, gain_ref, q_ref, ag_ref, k_ref, v_ref, o_ref, *, lam_init):
    tq = q_ref.shape[1]
    for h in range(N_HEADS):
        cols = slice(h * V_DIM, (h + 1) * V_DIM)
        qs = _split_halves(q_ref[0, :, cols] * ATTN_SCALE)
        s = lax.dot_general(qs, k_ref[0, :, cols], (((1,), (1,)), ((), ())), preferred_element_type=F32)
        p = jnp.exp(s - s.max(axis=-1, keepdims=True))
        den = p.sum(axis=-1, keepdims=True)
        den1, den2 = den[:tq], den[tq:]
        ratio = lam_ref[:, 0:1] * den1 / den2
        a = (p[:tq] - ratio * p[tq:]).astype(BF16)
        o = jnp.dot(a, v_ref[0, :, cols], preferred_element_type=F32) / den1
        y = o * lax.rsqrt(jnp.mean(o * o, axis=-1, keepdims=True) + RMS_EPS) * (gain_ref[...] * (1.0 - lam_init))
        o_ref[0, :, cols] = (_silu(ag_ref[0, :, cols].astype(F32)) * y).astype(BF16)


def _attention_ctx(lam, gain, p_c, lam_init):
    nb, t, _ = p_c.shape
    col = lambda off: (lambda b: (b, 0, off // D_ATTN))
    return pl.pallas_call(
        functools.partial(_attn_ctx_kernel, lam_init=lam_init),
        grid=(nb,),
        in_specs=[
            pl.BlockSpec((1, LANES), lambda b: (0, 0)),
            pl.BlockSpec((1, V_DIM), lambda b: (0, 0)),
            pl.BlockSpec((1, t, D_ATTN), col(Q_OFF)),
            pl.BlockSpec((1, t, D_ATTN), col(G_OFF)),
            pl.BlockSpec((1, t, D_ATTN), col(K_OFF)),
            pl.BlockSpec((1, t, D_ATTN), col(V_OFF)),
        ],
        out_specs=pl.BlockSpec((1, t, D_ATTN), col(0)),
        out_shape=jax.ShapeDtypeStruct((nb, t, D_ATTN), BF16),
        compiler_params=_params("arbitrary"),
        name="diff_attn_ctx",
    )(lam, gain, p_c, p_c, p_c, p_c)


def _attn_kernel(lam_ref, gain_ref, q_ref, ag_ref, kc_ref, kl_ref, vc_ref, vl_ref, o_ref,
                 s0, s1, m0, m1, p0, p1, l0, l1, *, lam_init):
    step = pl.program_id(0)
    tq = q_ref.shape[1] // 2
    nt = (((1,), (1,)), ((), ()))
    tn = (((0,), (0,)), ((), ()))
    chunks = ([(kc_ref, vc_ref, r) for r in range(0, kc_ref.shape[1], ATTN_CHUNK)]
              + [(kl_ref, vl_ref, r) for r in range(0, kl_ref.shape[1], ATTN_CHUNK)])
    groups = ATTN_CHUNK // SUBLANES

    @pl.when(step == 0)
    def _():
        s1[...] = jnp.zeros_like(s1)
        m1[...] = jnp.zeros_like(m1)
        p0[...] = jnp.zeros_like(p0)
        l0[...] = jnp.ones_like(l0)

    def half(tile_rows, s_w, m_w, s_r, m_r, p_w, l_w, p_r, l_r):
        qs = _split_halves(q_ref[0, tile_rows, :])
        m_prev = m_r[...]
        den = l_r[...]
        den1, den2 = den[:, :tq], den[:, tq:]
        ratio = (lam_ref[:, 0:1] * den1 / den2).astype(BF16)
        m_acc = jnp.full((SUBLANES, 2 * tq), -jnp.inf, F32)
        l_acc = jnp.zeros((SUBLANES, 2 * tq), F32)
        o_t = jnp.zeros((V_DIM, tq), F32)
        scores = {id(r): lax.dot_general(r[0], qs, nt, preferred_element_type=F32) for r in (kc_ref, kl_ref)}
        for c, (k_ref, v_ref, r0) in enumerate(chunks):
            rows = slice(c * ATTN_CHUNK, (c + 1) * ATTN_CHUNK)
            s = scores[id(k_ref)][r0:r0 + ATTN_CHUNK]
            s_w[rows, :] = s
            pace = jnp.where(l_acc < 0.0, 1.0, 0.0)
            m_acc = jnp.maximum(m_acc + pace, s.reshape(groups, SUBLANES, 2 * tq).max(axis=0))
            p = jnp.exp2(s_r[rows, :] - m_prev)
            l_acc = l_acc + p.reshape(groups, SUBLANES, 2 * tq).sum(axis=0)
            p_w[rows, :] = p.astype(BF16)
            pr = p_r[rows, :]
            a = pr[:, :tq] - ratio * pr[:, tq:]
            o_t = o_t + lax.dot_general(v_ref[0, r0:r0 + ATTN_CHUNK, :], a, tn, preferred_element_type=F32)
        m_w[...] = m_acc.max(axis=0, keepdims=True)
        l_w[...] = l_acc.sum(axis=0, keepdims=True)
        o = (o_t * (1.0 / den1)).T
        y = o * lax.rsqrt(jnp.mean(o * o, axis=-1, keepdims=True) + RMS_EPS) * (gain_ref[...] * (1.0 - lam_init))
        o_ref[0, tile_rows, :] = (_silu(ag_ref[0, tile_rows, :].astype(F32)) * y).astype(BF16)

    half(slice(0, tq), s0, m0, s1, m1, p1, l1, p0, l0)
    half(slice(tq, 2 * tq), s1, m1, s0, m0, p0, l0, p1, l1)


def _attention(lam, gain, p_l, p_c, kc_off, vc_off, tq, lam_init):
    nb, seq, _ = p_l.shape
    n_ctx = p_c.shape[1]
    assert n_ctx % ATTN_CHUNK == 0 and seq % ATTN_CHUNK == 0 and seq % (2 * tq) == 0
    n_keys = n_ctx + seq
    pairs_per_head = seq // (2 * tq)
    n_pairs = nb * N_HEADS * pairs_per_head

    def pair(k):
        k = jnp.clip(k, 0, n_pairs - 1)
        return k // (N_HEADS * pairs_per_head), (k // pairs_per_head) % N_HEADS, k % pairs_per_head

    def tile_map(off, lag):
        def f(k):
            b, h, i = pair(k - lag)
            return b, i, off // V_DIM + h
        return f

    def head_map(off, lag):
        def f(k):
            b, h, _ = pair(k - lag)
            return b, 0, off // V_DIM + h
        return f

    return pl.pallas_call(
        functools.partial(_attn_kernel, lam_init=lam_init),
        grid=(n_pairs + 1,),
        in_specs=[
            pl.BlockSpec((1, LANES), lambda k: (0, 0)),
            pl.BlockSpec((1, V_DIM), lambda k: (0, 0)),
            pl.BlockSpec((1, 2 * tq, V_DIM), tile_map(Q_OFF, 0)),
            pl.BlockSpec((1, 2 * tq, V_DIM), tile_map(G_OFF, 1)),
            pl.BlockSpec((1, n_ctx, V_DIM), head_map(kc_off, 0)),
            pl.BlockSpec((1, seq, V_DIM), head_map(K_OFF, 0)),
            pl.BlockSpec((1, n_ctx, V_DIM), head_map(vc_off, 1)),
            pl.BlockSpec((1, seq, V_DIM), head_map(V_OFF, 1)),
        ],
        out_specs=pl.BlockSpec((1, 2 * tq, V_DIM), tile_map(0, 1)),
        out_shape=jax.ShapeDtypeStruct((nb, seq, D_ATTN), BF16),
        scratch_shapes=[
            pltpu.VMEM((n_keys, 2 * tq), F32), pltpu.VMEM((n_keys, 2 * tq), F32),
            pltpu.VMEM((1, 2 * tq), F32), pltpu.VMEM((1, 2 * tq), F32),
            pltpu.VMEM((n_keys, 2 * tq), BF16), pltpu.VMEM((n_keys, 2 * tq), BF16),
            pltpu.VMEM((1, 2 * tq), F32), pltpu.VMEM((1, 2 * tq), F32),
        ],
        compiler_params=_params("arbitrary"),
        name="diff_attn",
    )(lam, gain, p_l, p_l, p_c, p_l, p_c, p_l)


def _mix_kernel(pc_ref, cup_ref, ccp_ref, cun_ref, ccn_ref, pp_ref, pup_ref, pun_ref, ya_ref, x_ref, g_ref,
                cw_ref, pw_ref, ps_ref, wo_ref, lg_ref, lb_ref, o_ref, mix_ref, *, seq_len):
    i = pl.program_id(1)
    tm = x_ref.shape[1]
    n_tiles = seq_len // tm
    has_prev = i > 0
    has_next = i < n_tiles - 1
    row = lax.broadcasted_iota(jnp.int32, (tm, 1), 0)

    cu = pc_ref[0, :, 0:D_CONV].astype(F32)
    cb = pc_ref[0, :, D_CONV:2 * D_CONV].astype(F32)
    cc = pc_ref[0, :, 2 * D_CONV:3 * D_CONV].astype(F32)
    cg = pc_ref[0, :, 3 * D_CONV:4 * D_CONV].astype(F32)
    v = cc * cu
    v_before = (ccp_ref[0].astype(F32) * cup_ref[0].astype(F32))[HALO - 1:HALO]
    v_after = (ccn_ref[0].astype(F32) * cun_ref[0].astype(F32))[0:1]
    v_before = jnp.where(has_prev, v_before, 0.0)
    v_after = jnp.where(has_next, v_after, 0.0)
    v_m1 = jnp.where(row == 0, v_before, pltpu.roll(v, 1, 0))
    v_p1 = jnp.where(row == tm - 1, v_after, pltpu.roll(v, tm - 1, 0))
    cw = cw_ref[...]
    y_conv = _silu(cg) * (cb * (v_m1 * cw[0:1] + v * cw[1:2] + v_p1 * cw[2:3]))
    mix_ref[:, 0:D_CONV] = y_conv.astype(BF16)

    mix_ref[:, D_CONV:D_CONV + D_ATTN] = ya_ref[0]

    u = pp_ref[0, :, 0:D_POOL].astype(F32)
    pg = pp_ref[0, :, D_POOL:2 * D_POOL].astype(F32)
    u_before = jnp.where(has_prev, pup_ref[0].astype(F32), 0.0)
    u_after = jnp.where(has_next, pun_ref[0].astype(F32), 0.0)
    u_ext = jnp.concatenate([u_before, u, u_after], axis=0)
    n_ext = tm + 2 * HALO
    t = i * tm + row
    for gi, w in enumerate(POOL_WINDOWS):
        sl = slice(gi * POOL_GROUP, (gi + 1) * POOL_GROUP)
        s = u_ext[:, sl]
        k = 1
        while k < w:
            s = s + pltpu.roll(s, n_ext - k, 0)
            k *= 2
        win = pltpu.roll(s, w // 2, 0)[HALO:HALO + tm]
        lo = jnp.maximum(t - w // 2, 0)
        hi = jnp.minimum(t + w - w // 2, seq_len)
        d = (win / (hi - lo).astype(F32) - u[:, sl]).astype(BF16)
        yg = jnp.dot(d, pw_ref[gi], preferred_element_type=F32)
        y_pool = _silu(pg[:, sl]) * (yg * ps_ref[:, sl])
        mix_ref[:, D_CONV + D_ATTN + gi * POOL_GROUP:D_CONV + D_ATTN + (gi + 1) * POOL_GROUP] = y_pool.astype(BF16)

    pace = None
    for r in range(0, tm, MIX_SUB):
        rs = slice(r, r + MIX_SUB)
        lhs = mix_ref[rs, :]
        if pace is not None:
            lhs = lhs + pace
        out = jnp.dot(lhs, wo_ref[...], preferred_element_type=F32)
        pace = jnp.where(out[0:1, :] < -jnp.inf, 1.0, 0.0).astype(BF16)
        z = DEEPNORM_ALPHA * x_ref[0, rs, :] + g_ref[0] * out
        mu = jnp.mean(z, axis=-1, keepdims=True)
        zc = z - mu
        var = jnp.mean(zc * zc, axis=-1, keepdims=True)
        o_ref[0, rs, :] = zc * lax.rsqrt(var + LN_EPS) * lg_ref[...] + lb_ref[...]


def _mix(p, y_attn, x, mods3, mod_row, conv_w, pool_w_bf16, pool_scale, w_out_bf16, ln_g, ln_b, tm):
    nb, t, _ = x.shape
    hb = tm // HALO
    last = t // HALO - 1
    prev = lambda i: jnp.maximum(i * hb - 1, 0)
    nxt = lambda i: jnp.minimum((i + 1) * hb, last)
    cc_blk = 2
    const2 = lambda b, i: (0, 0)
    in_specs = [
        pl.BlockSpec((1, tm, 4 * D_CONV), lambda b, i: (b, i, 0)),
        pl.BlockSpec((1, HALO, D_CONV), lambda b, i: (b, prev(i), 0)),
        pl.BlockSpec((1, HALO, D_CONV), lambda b, i: (b, prev(i), cc_blk)),
        pl.BlockSpec((1, HALO, D_CONV), lambda b, i: (b, nxt(i), 0)),
        pl.BlockSpec((1, HALO, D_CONV), lambda b, i: (b, nxt(i), cc_blk)),
        pl.BlockSpec((1, tm, 2 * D_POOL), lambda b, i: (b, i, POOL_OFF // (2 * D_POOL))),
        pl.BlockSpec((1, HALO, D_POOL), lambda b, i: (b, prev(i), POOL_OFF // D_POOL)),
        pl.BlockSpec((1, HALO, D_POOL), lambda b, i: (b, nxt(i), POOL_OFF // D_POOL)),
        pl.BlockSpec((1, tm, D_ATTN), lambda b, i: (b, i, 0)),
        pl.BlockSpec((1, tm, D_MODEL), lambda b, i: (b, i, 0)),
        pl.BlockSpec((1, 1, D_MODEL), lambda b, i: (mod_row(b), 0, 2)),
        pl.BlockSpec(conv_w.shape, const2),
        pl.BlockSpec(pool_w_bf16.shape, lambda b, i: (0, 0, 0)),
        pl.BlockSpec(pool_scale.shape, const2),
        pl.BlockSpec(w_out_bf16.shape, const2, pipeline_mode=pl.Buffered(1)),
        pl.BlockSpec(ln_g.shape, const2),
        pl.BlockSpec(ln_b.shape, const2),
    ]
    return pl.pallas_call(
        functools.partial(_mix_kernel, seq_len=t),
        grid=(nb, t // tm),
        in_specs=in_specs,
        out_specs=pl.BlockSpec((1, tm, D_MODEL), lambda b, i: (b, i, 0)),
        out_shape=jax.ShapeDtypeStruct((nb, t, D_MODEL), F32),
        scratch_shapes=[pltpu.VMEM((tm, D_MODEL), BF16)],
        compiler_params=_params("arbitrary", "arbitrary"),
        name="mix_out_ln",
    )(p, p, p, p, p, p, p, p, y_attn, x, mods3, conv_w, pool_w_bf16, pool_scale, w_out_bf16, ln_g, ln_b)


def _rope_tables(seq_len):
    n = QK_DIM // 4
    inv = ROPE_THETA ** (-np.arange(n, dtype=np.float64) / n)
    tok = np.arange(seq_len)
    pos = np.stack([tok // GRID_W, tok % GRID_W], axis=1).astype(np.float64)
    m = np.arange(LANES)
    ang = pos[:, (m // n) % 2] * inv[m % n][None, :]
    second = (m >= LANES // 2)[None, :]
    cos, sin = np.cos(ang), np.sin(ang)
    return jnp.asarray(cos, dtype=F32), jnp.asarray(np.where(second, sin, -sin), dtype=F32)


def kernel(x, c, ctx, c_ctx, w_mod, b_mod, w_in, conv_w, lam_q1, lam_k1, lam_q2, lam_k2, subln_g, pool_w,
           pool_scale, w_out, ln_g, ln_b):
    nb, seq, _ = x.shape
    ctx_len = ctx.shape[1]
    assert nb + 1 <= MOD_ROWS

    cvec = jnp.concatenate([c, c_ctx[None, :], jnp.zeros((MOD_ROWS - nb - 1, D_MODEL), F32)], axis=0)
    lamv = jnp.stack([lam_q1, lam_k1, lam_q2, lam_k2], axis=1).astype(F32)
    mods, lams = _modulation(cvec, w_mod, b_mod, lamv)
    tables = _rope_tables(seq)
    latent_row = lambda b: b
    ctx_row = lambda b: nb

    xl, xc = x, ctx
    for l in range(DEPTH):
        last = l == DEPTH - 1
        lam_init = 0.8 - 0.6 * math.exp(-0.3 * l)
        mods3 = mods[l].reshape(MOD_ROWS, 1, 3 * D_MODEL)
        w_in_l = _head_layout(w_in[l].astype(BF16))
        w_out_l = w_out[l].astype(BF16)
        pool_w_l = pool_w[l].astype(BF16)
        gain = subln_g[l].reshape(1, V_DIM)
        p_l = _project(xl, mods3, latent_row, w_in_l, 0, D_IN // PROJ_TN, PROJ_TM, tables)
        xc_flat = xc.reshape(1, nb * ctx_len, D_MODEL)
        if last:
            p_c = _project(xc_flat, mods3, ctx_row, w_in_l, K_OFF // PROJ_TN, 2 * D_ATTN // PROJ_TN,
                           nb * ctx_len, None).reshape(nb, ctx_len, 2 * D_ATTN)
            kc_off, vc_off = 0, D_ATTN
        else:
            p_c = _project(xc_flat, mods3, ctx_row, w_in_l, 0, D_IN // PROJ_TN, nb * ctx_len,
                           None).reshape(nb, ctx_len, D_IN)
            kc_off, vc_off = K_OFF, V_OFF
        y_attn = _attention(lams[l], gain, p_l, p_c, kc_off, vc_off, ATTN_TQ, lam_init)
        mix_args = (conv_w[l], pool_w_l, pool_scale[l].reshape(1, D_POOL), w_out_l,
                    ln_g[l].reshape(1, D_MODEL), ln_b[l].reshape(1, D_MODEL))
        xl_new = _mix(p_l, y_attn, xl, mods3, latent_row, *mix_args, MIX_TM)
        if not last:
            y_attn_c = _attention_ctx(lams[l], gain, p_c, lam_init)
            xc = _mix(p_c, y_attn_c, xc, mods3, ctx_row, *mix_args, ctx_len)
        xl = xl_new
    return xl
```

```python
import functools
import math

import numpy as np
import jax
import jax.numpy as jnp
from jax import lax
from jax.experimental import pallas as pl
from jax.experimental.pallas import tpu as pltpu

F32 = jnp.float32
BF16 = jnp.bfloat16

D_MODEL = 2048
DEPTH = 2
GRID_W = 64
D_CONV = 512
D_ATTN = 1024
D_POOL = 512
N_HEADS = 8
V_DIM = 128
QK_DIM = 64
ATTN_SCALE = QK_DIM ** -0.5
LOG2_E = math.log2(math.e)
Q_PRESCALE = ATTN_SCALE * LOG2_E
POOL_WINDOWS = (2, 4, 8, 16)
POOL_GROUP = 128
ROPE_THETA = 10000.0
LN_EPS = 1e-5
RMS_EPS = 1e-5
DEEPNORM_ALPHA = (2 * DEPTH) ** 0.25
D_IN = 4 * D_CONV + 4 * D_ATTN + 2 * D_POOL
Q_OFF = 4 * D_CONV
K_OFF = Q_OFF + D_ATTN
V_OFF = K_OFF + D_ATTN
G_OFF = V_OFF + D_ATTN
POOL_OFF = G_OFF + D_ATTN

LANES = 128
SUBLANES = 8
BF16_ROWS = 16
HALO = BF16_ROWS
MOD_ROWS = 8
VMEM_LIMIT = 56 * 1024 * 1024

PROJ_TM = 1024
PROJ_TN = 1024
ATTN_TQ = 256
ATTN_CHUNK = 256
MIX_TM = 512
MIX_SUB = 256
MOD_TN = 1536


def _silu(x):
    return x / (1.0 + jnp.exp(-x))


def _params(*sem):
    return pltpu.CompilerParams(dimension_semantics=sem, vmem_limit_bytes=VMEM_LIMIT)


def _mod_kernel(c_ref, w_ref, b_ref, lamv_ref, mod_ref, lam_ref):
    layer = pl.program_id(0)
    s = _silu(c_ref[...]).astype(BF16)
    w = w_ref[0].astype(BF16)
    mod_ref[0] = jnp.dot(s, w, preferred_element_type=F32) + b_ref[0]
    lv = lamv_ref[0]
    d1 = jnp.sum(lv[0:1] * lv[1:2], axis=-1, keepdims=True)
    d2 = jnp.sum(lv[2:3] * lv[3:4], axis=-1, keepdims=True)
    lf = jnp.zeros((1, LANES), F32) + layer.astype(F32)
    lam_init = 0.8 - 0.6 * jnp.exp(-0.3 * lf)
    lam_ref[0] = jnp.exp(d1) - jnp.exp(d2) + lam_init


def _modulation(cvec, w_mod, b_mod, lamv):
    n_out = w_mod.shape[-1]
    return pl.pallas_call(
        _mod_kernel,
        grid=(DEPTH, n_out // MOD_TN),
        in_specs=[
            pl.BlockSpec((MOD_ROWS, D_MODEL), lambda l, j: (0, 0)),
            pl.BlockSpec((1, D_MODEL, MOD_TN), lambda l, j: (l, 0, j)),
            pl.BlockSpec((1, 1, MOD_TN), lambda l, j: (l, 0, j)),
            pl.BlockSpec((1, 4, QK_DIM), lambda l, j: (l, 0, 0)),
        ],
        out_specs=[
            pl.BlockSpec((1, MOD_ROWS, MOD_TN), lambda l, j: (l, 0, j)),
            pl.BlockSpec((1, 1, LANES), lambda l, j: (l, 0, 0)),
        ],
        out_shape=[
            jax.ShapeDtypeStruct((DEPTH, MOD_ROWS, n_out), F32),
            jax.ShapeDtypeStruct((DEPTH, 1, LANES), F32),
        ],
        compiler_params=_params("arbitrary", "arbitrary"),
        name="modulation",
    )(cvec, w_mod, b_mod.reshape(DEPTH, 1, n_out), lamv)


def _proj_kernel(*refs, rope, q_blk, k_blk):
    if rope:
        x_ref, sh_ref, sc_ref, w_ref, cos_ref, sin_ref, o_ref, h_ref = refs
    else:
        x_ref, sh_ref, sc_ref, w_ref, o_ref, h_ref = refs
    j = pl.program_id(2)

    @pl.when(j == 0)
    def _():
        h_ref[...] = (x_ref[0] * (1.0 + sc_ref[0]) + sh_ref[0]).astype(BF16)

    acc = jnp.dot(h_ref[...], w_ref[...], preferred_element_type=F32)
    o_ref[0] = acc.astype(BF16)

    if rope:
        @pl.when((j == q_blk) | (j == k_blk))
        def _():
            scale = jnp.where(j == q_blk, Q_PRESCALE, 1.0).astype(F32)
            cos = cos_ref[...] * scale
            sin = sin_ref[...] * scale
            for c in range(acc.shape[1] // LANES):
                xc = acc[:, c * LANES:(c + 1) * LANES]
                r = xc * cos + pltpu.roll(xc, LANES // 2, 1) * sin
                o_ref[0, :, c * LANES:(c + 1) * LANES] = r.astype(BF16)


def _project(x, mods3, mod_row, w_bf16, col_blk0, n_col_blks, tm, rope_tables):
    nb, t, _ = x.shape
    tn = PROJ_TN
    rope = rope_tables is not None
    in_specs = [
        pl.BlockSpec((1, tm, D_MODEL), lambda b, i, j: (b, i, 0)),
        pl.BlockSpec((1, 1, D_MODEL), lambda b, i, j: (mod_row(b), 0, 0)),
        pl.BlockSpec((1, 1, D_MODEL), lambda b, i, j: (mod_row(b), 0, 1)),
        pl.BlockSpec((D_MODEL, tn), lambda b, i, j: (0, j + col_blk0)),
    ]
    args = [x, mods3, mods3, w_bf16]
    if rope:
        in_specs += [pl.BlockSpec((tm, LANES), lambda b, i, j: (i, 0))] * 2
        args += list(rope_tables)
    return pl.pallas_call(
        functools.partial(_proj_kernel, rope=rope, q_blk=Q_OFF // tn - col_blk0, k_blk=K_OFF // tn - col_blk0),
        grid=(nb, t // tm, n_col_blks),
        in_specs=in_specs,
        out_specs=pl.BlockSpec((1, tm, tn), lambda b, i, j: (b, i, j)),
        out_shape=jax.ShapeDtypeStruct((nb, t, n_col_blks * tn), BF16),
        scratch_shapes=[pltpu.VMEM((tm, D_MODEL), BF16)],
        compiler_params=_params("arbitrary", "arbitrary", "arbitrary"),
        name="proj_rope" if rope else "proj",
    )(*args)


def _head_layout(w):
    d = w.shape[0]
    n_freq = QK_DIM // 4
    qk = w[:, Q_OFF:V_OFF].reshape(d, 2, N_HEADS, 2, 2, 2, n_freq)
    qk = qk.transpose(0, 1, 2, 5, 3, 4, 6).reshape(d, 2 * D_ATTN)
    return lax.dynamic_update_slice(w, qk, (0, Q_OFF))


def _split_halves(q):
    lane = lax.broadcasted_iota(jnp.int32, (1, LANES), 1)
    first = (lane // (QK_DIM // 2)) % 2 == 0
    zero = jnp.zeros_like(q)
    return jnp.concatenate([jnp.where(first, q, zero), jnp.where(first, zero, q)], axis=0)


def _attn_ctx_kernel(lam_ref, gain_ref, q_ref, ag_ref, k_ref, v_ref, o_ref, *, lam_init):
    tq = q_ref.shape[1]
    for h in range(N_HEADS):
        cols = slice(h * V_DIM, (h + 1) * V_DIM)
        qs = _split_halves(q_ref[0, :, cols] * ATTN_SCALE)
        s = lax.dot_general(qs, k_ref[0, :, cols], (((1,), (1,)), ((), ())), preferred_element_type=F32)
        p = jnp.exp(s - s.max(axis=-1, keepdims=True))
        den = p.sum(axis=-1, keepdims=True)
        den1, den2 = den[:tq], den[tq:]
        ratio = lam_ref[:, 0:1] * den1 / den2
        a = (p[:tq] - ratio * p[tq:]).astype(BF16)
        o = jnp.dot(a, v_ref[0, :, cols], preferred_element_type=F32) / den1
        y = o * lax.rsqrt(jnp.mean(o * o, axis=-1, keepdims=True) + RMS_EPS) * (gain_ref[...] * (1.0 - lam_init))
        o_ref[0, :, cols] = (_silu(ag_ref[0, :, cols].astype(F32)) * y).astype(BF16)


def _attention_ctx(lam, gain, p_c, lam_init):
    nb, t, _ = p_c.shape
    col = lambda off: (lambda b: (b, 0, off // D_ATTN))
    return pl.pallas_call(
        functools.partial(_attn_ctx_kernel, lam_init=lam_init),
        grid=(nb,),
        in_specs=[
            pl.BlockSpec((1, LANES), lambda b: (0, 0)),
            pl.BlockSpec((1, V_DIM), lambda b: (0, 0)),
            pl.BlockSpec((1, t, D_ATTN), col(Q_OFF)),
            pl.BlockSpec((1, t, D_ATTN), col(G_OFF)),
            pl.BlockSpec((1, t, D_ATTN), col(K_OFF)),
            pl.BlockSpec((1, t, D_ATTN), col(V_OFF)),
        ],
        out_specs=pl.BlockSpec((1, t, D_ATTN), col(0)),
        out_shape=jax.ShapeDtypeStruct((nb, t, D_ATTN), BF16),
        compiler_params=_params("arbitrary"),
        name="diff_attn_ctx",
    )(lam, gain, p_c, p_c, p_c, p_c)


def _attn_kernel(lam_ref, gain_ref, q_ref, ag_ref, kc_ref, kl_ref, vc_ref, vl_ref, o_ref,
                 s0, s1, m0, m1, p0, p1, l0, l1, *, lam_init):
    step = pl.program_id(0)
    tq = q_ref.shape[1] // 2
    nt = (((1,), (1,)), ((), ()))
    tn = (((0,), (0,)), ((), ()))
    chunks = ([(kc_ref, vc_ref, r) for r in range(0, kc_ref.shape[1], ATTN_CHUNK)]
              + [(kl_ref, vl_ref, r) for r in range(0, kl_ref.shape[1], ATTN_CHUNK)])
    groups = ATTN_CHUNK // SUBLANES

    @pl.when(step == 0)
    def _():
        s1[...] = jnp.zeros_like(s1)
        m1[...] = jnp.zeros_like(m1)
        p0[...] = jnp.zeros_like(p0)
        l0[...] = jnp.ones_like(l0)

    def half(tile_rows, s_w, m_w, s_r, m_r, p_w, l_w, p_r, l_r):
        qs = _split_halves(q_ref[0, tile_rows, :])
        m_prev = m_r[...]
        den = l_r[...]
        den1, den2 = den[:, :tq], den[:, tq:]
        ratio = (lam_ref[:, 0:1] * den1 / den2).astype(BF16)
        m_acc = jnp.full((SUBLANES, 2 * tq), -jnp.inf, F32)
        l_acc = jnp.zeros((SUBLANES, 2 * tq), F32)
        o_t = jnp.zeros((V_DIM, tq), F32)
        for c, (k_ref, v_ref, r0) in enumerate(chunks):
            rows = slice(c * ATTN_CHUNK, (c + 1) * ATTN_CHUNK)
            s = lax.dot_general(k_ref[0, r0:r0 + ATTN_CHUNK, :], qs, nt, preferred_element_type=F32)
            s_w[rows, :] = s
            pace = jnp.where(l_acc < 0.0, 1.0, 0.0)
            m_acc = jnp.maximum(m_acc + pace, s.reshape(groups, SUBLANES, 2 * tq).max(axis=0))
            for u in range(c * ATTN_CHUNK, (c + 1) * ATTN_CHUNK, BF16_ROWS):
                p = jnp.exp2(s_r[u:u + BF16_ROWS, :] - m_prev)
                l_acc = l_acc + p[:SUBLANES] + p[SUBLANES:]
                p_w[u:u + BF16_ROWS, :] = p.astype(BF16)
            pr = p_r[rows, :]
            a = pr[:, :tq] - ratio * pr[:, tq:]
            o_t = o_t + lax.dot_general(v_ref[0, r0:r0 + ATTN_CHUNK, :], a, tn, preferred_element_type=F32)
        m_w[...] = m_acc.max(axis=0, keepdims=True)
        l_w[...] = l_acc.sum(axis=0, keepdims=True)
        o = (o_t * (1.0 / den1)).T
        y = o * lax.rsqrt(jnp.mean(o * o, axis=-1, keepdims=True) + RMS_EPS) * (gain_ref[...] * (1.0 - lam_init))
        o_ref[0, tile_rows, :] = (_silu(ag_ref[0, tile_rows, :].astype(F32)) * y).astype(BF16)

    half(slice(0, tq), s0, m0, s1, m1, p1, l1, p0, l0)
    half(slice(tq, 2 * tq), s1, m1, s0, m0, p0, l0, p1, l1)


def _attention(lam, gain, p_l, p_c, kc_off, vc_off, tq, lam_init):
    nb, seq, _ = p_l.shape
    n_ctx = p_c.shape[1]
    assert n_ctx % ATTN_CHUNK == 0 and seq % ATTN_CHUNK == 0 and seq % (2 * tq) == 0
    n_keys = n_ctx + seq
    pairs_per_head = seq // (2 * tq)
    n_pairs = nb * N_HEADS * pairs_per_head

    def pair(k):
        k = jnp.clip(k, 0, n_pairs - 1)
        return k // (N_HEADS * pairs_per_head), (k // pairs_per_head) % N_HEADS, k % pairs_per_head

    def tile_map(off, lag):
        def f(k):
            b, h, i = pair(k - lag)
            return b, i, off // V_DIM + h
        return f

    def head_map(off, lag):
        def f(k):
            b, h, _ = pair(k - lag)
            return b, 0, off // V_DIM + h
        return f

    return pl.pallas_call(
        functools.partial(_attn_kernel, lam_init=lam_init),
        grid=(n_pairs + 1,),
        in_specs=[
            pl.BlockSpec((1, LANES), lambda k: (0, 0)),
            pl.BlockSpec((1, V_DIM), lambda k: (0, 0)),
            pl.BlockSpec((1, 2 * tq, V_DIM), tile_map(Q_OFF, 0)),
            pl.BlockSpec((1, 2 * tq, V_DIM), tile_map(G_OFF, 1)),
            pl.BlockSpec((1, n_ctx, V_DIM), head_map(kc_off, 0)),
            pl.BlockSpec((1, seq, V_DIM), head_map(K_OFF, 0)),
            pl.BlockSpec((1, n_ctx, V_DIM), head_map(vc_off, 1)),
            pl.BlockSpec((1, seq, V_DIM), head_map(V_OFF, 1)),
        ],
        out_specs=pl.BlockSpec((1, 2 * tq, V_DIM), tile_map(0, 1)),
        out_shape=jax.ShapeDtypeStruct((nb, seq, D_ATTN), BF16),
        scratch_shapes=[
            pltpu.VMEM((n_keys, 2 * tq), F32), pltpu.VMEM((n_keys, 2 * tq), F32),
            pltpu.VMEM((1, 2 * tq), F32), pltpu.VMEM((1, 2 * tq), F32),
            pltpu.VMEM((n_keys, 2 * tq), BF16), pltpu.VMEM((n_keys, 2 * tq), BF16),
            pltpu.VMEM((1, 2 * tq), F32), pltpu.VMEM((1, 2 * tq), F32),
        ],
        compiler_params=_params("arbitrary"),
        name="diff_attn",
    )(lam, gain, p_l, p_l, p_c, p_l, p_c, p_l)


def _mix_kernel(pc_ref, cup_ref, ccp_ref, cun_ref, ccn_ref, pp_ref, pup_ref, pun_ref, ya_ref, x_ref, g_ref,
                cw_ref, pw_ref, ps_ref, wo_ref, lg_ref, lb_ref, o_ref, mix_ref, *, seq_len):
    i = pl.program_id(1)
    tm = x_ref.shape[1]
    n_tiles = seq_len // tm
    has_prev = i > 0
    has_next = i < n_tiles - 1
    row = lax.broadcasted_iota(jnp.int32, (tm, 1), 0)

    cu = pc_ref[0, :, 0:D_CONV].astype(F32)
    cb = pc_ref[0, :, D_CONV:2 * D_CONV].astype(F32)
    cc = pc_ref[0, :, 2 * D_CONV:3 * D_CONV].astype(F32)
    cg = pc_ref[0, :, 3 * D_CONV:4 * D_CONV].astype(F32)
    v = cc * cu
    v_before = (ccp_ref[0].astype(F32) * cup_ref[0].astype(F32))[HALO - 1:HALO]
    v_after = (ccn_ref[0].astype(F32) * cun_ref[0].astype(F32))[0:1]
    v_before = jnp.where(has_prev, v_before, 0.0)
    v_after = jnp.where(has_next, v_after, 0.0)
    v_m1 = jnp.where(row == 0, v_before, pltpu.roll(v, 1, 0))
    v_p1 = jnp.where(row == tm - 1, v_after, pltpu.roll(v, tm - 1, 0))
    cw = cw_ref[...]
    y_conv = _silu(cg) * (cb * (v_m1 * cw[0:1] + v * cw[1:2] + v_p1 * cw[2:3]))
    mix_ref[:, 0:D_CONV] = y_conv.astype(BF16)

    mix_ref[:, D_CONV:D_CONV + D_ATTN] = ya_ref[0]

    u = pp_ref[0, :, 0:D_POOL].astype(F32)
    pg = pp_ref[0, :, D_POOL:2 * D_POOL].astype(F32)
    u_before = jnp.where(has_prev, pup_ref[0].astype(F32), 0.0)
    u_after = jnp.where(has_next, pun_ref[0].astype(F32), 0.0)
    u_ext = jnp.concatenate([u_before, u, u_after], axis=0)
    n_ext = tm + 2 * HALO
    t = i * tm + row
    for gi, w in enumerate(POOL_WINDOWS):
        sl = slice(gi * POOL_GROUP, (gi + 1) * POOL_GROUP)
        s = u_ext[:, sl]
        k = 1
        while k < w:
            s = s + pltpu.roll(s, n_ext - k, 0)
            k *= 2
        win = pltpu.roll(s, w // 2, 0)[HALO:HALO + tm]
        lo = jnp.maximum(t - w // 2, 0)
        hi = jnp.minimum(t + w - w // 2, seq_len)
        d = (win / (hi - lo).astype(F32) - u[:, sl]).astype(BF16)
        yg = jnp.dot(d, pw_ref[gi], preferred_element_type=F32)
        y_pool = _silu(pg[:, sl]) * (yg * ps_ref[:, sl])
        mix_ref[:, D_CONV + D_ATTN + gi * POOL_GROUP:D_CONV + D_ATTN + (gi + 1) * POOL_GROUP] = y_pool.astype(BF16)

    pace = None
    for r in range(0, tm, MIX_SUB):
        rs = slice(r, r + MIX_SUB)
        lhs = mix_ref[rs, :]
        if pace is not None:
            lhs = lhs + pace
        out = jnp.dot(lhs, wo_ref[...], preferred_element_type=F32)
        pace = jnp.where(out[0:1, :] < -jnp.inf, 1.0, 0.0).astype(BF16)
        z = DEEPNORM_ALPHA * x_ref[0, rs, :] + g_ref[0] * out
        mu = jnp.mean(z, axis=-1, keepdims=True)
        zc = z - mu
        var = jnp.mean(zc * zc, axis=-1, keepdims=True)
        o_ref[0, rs, :] = zc * lax.rsqrt(var + LN_EPS) * lg_ref[...] + lb_ref[...]


def _mix(p, y_attn, x, mods3, mod_row, conv_w, pool_w_bf16, pool_scale, w_out_bf16, ln_g, ln_b, tm):
    nb, t, _ = x.shape
    hb = tm // HALO
    last = t // HALO - 1
    prev = lambda i: jnp.maximum(i * hb - 1, 0)
    nxt = lambda i: jnp.minimum((i + 1) * hb, last)
    cc_blk = 2
    const2 = lambda b, i: (0, 0)
    in_specs = [
        pl.BlockSpec((1, tm, 4 * D_CONV), lambda b, i: (b, i, 0)),
        pl.BlockSpec((1, HALO, D_CONV), lambda b, i: (b, prev(i), 0)),
        pl.BlockSpec((1, HALO, D_CONV), lambda b, i: (b, prev(i), cc_blk)),
        pl.BlockSpec((1, HALO, D_CONV), lambda b, i: (b, nxt(i), 0)),
        pl.BlockSpec((1, HALO, D_CONV), lambda b, i: (b, nxt(i), cc_blk)),
        pl.BlockSpec((1, tm, 2 * D_POOL), lambda b, i: (b, i, POOL_OFF // (2 * D_POOL))),
        pl.BlockSpec((1, HALO, D_POOL), lambda b, i: (b, prev(i), POOL_OFF // D_POOL)),
        pl.BlockSpec((1, HALO, D_POOL), lambda b, i: (b, nxt(i), POOL_OFF // D_POOL)),
        pl.BlockSpec((1, tm, D_ATTN), lambda b, i: (b, i, 0)),
        pl.BlockSpec((1, tm, D_MODEL), lambda b, i: (b, i, 0)),
        pl.BlockSpec((1, 1, D_MODEL), lambda b, i: (mod_row(b), 0, 2)),
        pl.BlockSpec(conv_w.shape, const2),
        pl.BlockSpec(pool_w_bf16.shape, lambda b, i: (0, 0, 0)),
        pl.BlockSpec(pool_scale.shape, const2),
        pl.BlockSpec(w_out_bf16.shape, const2, pipeline_mode=pl.Buffered(1)),
        pl.BlockSpec(ln_g.shape, const2),
        pl.BlockSpec(ln_b.shape, const2),
    ]
    return pl.pallas_call(
        functools.partial(_mix_kernel, seq_len=t),
        grid=(nb, t // tm),
        in_specs=in_specs,
        out_specs=pl.BlockSpec((1, tm, D_MODEL), lambda b, i: (b, i, 0)),
        out_shape=jax.ShapeDtypeStruct((nb, t, D_MODEL), F32),
        scratch_shapes=[pltpu.VMEM((tm, D_MODEL), BF16)],
        compiler_params=_params("arbitrary", "arbitrary"),
        name="mix_out_ln",
    )(p, p, p, p, p, p, p, p, y_attn, x, mods3, conv_w, pool_w_bf16, pool_scale, w_out_bf16, ln_g, ln_b)


def _rope_tables(seq_len):
    n = QK_DIM // 4
    inv = ROPE_THETA ** (-np.arange(n, dtype=np.float64) / n)
    tok = np.arange(seq_len)
    pos = np.stack([tok // GRID_W, tok % GRID_W], axis=1).astype(np.float64)
    m = np.arange(LANES)
    ang = pos[:, (m // n) % 2] * inv[m % n][None, :]
    second = (m >= LANES // 2)[None, :]
    cos, sin = np.cos(ang), np.sin(ang)
    return jnp.asarray(cos, dtype=F32), jnp.asarray(np.where(second, sin, -sin), dtype=F32)


def kernel(x, c, ctx, c_ctx, w_mod, b_mod, w_in, conv_w, lam_q1, lam_k1, lam_q2, lam_k2, subln_g, pool_w,
           pool_scale, w_out, ln_g, ln_b):
    nb, seq, _ = x.shape
    ctx_len = ctx.shape[1]
    assert nb + 1 <= MOD_ROWS

    cvec = jnp.concatenate([c, c_ctx[None, :], jnp.zeros((MOD_ROWS - nb - 1, D_MODEL), F32)], axis=0)
    lamv = jnp.stack([lam_q1, lam_k1, lam_q2, lam_k2], axis=1).astype(F32)
    mods, lams = _modulation(cvec, w_mod, b_mod, lamv)
    tables = _rope_tables(seq)
    latent_row = lambda b: b
    ctx_row = lambda b: nb

    xl, xc = x, ctx
    for l in range(DEPTH):
        last = l == DEPTH - 1
        lam_init = 0.8 - 0.6 * math.exp(-0.3 * l)
        mods3 = mods[l].reshape(MOD_ROWS, 1, 3 * D_MODEL)
        w_in_l = _head_layout(w_in[l].astype(BF16))
        w_out_l = w_out[l].astype(BF16)
        pool_w_l = pool_w[l].astype(BF16)
        gain = subln_g[l].reshape(1, V_DIM)
        p_l = _project(xl, mods3, latent_row, w_in_l, 0, D_IN // PROJ_TN, PROJ_TM, tables)
        xc_flat = xc.reshape(1, nb * ctx_len, D_MODEL)
        if last:
            p_c = _project(xc_flat, mods3, ctx_row, w_in_l, K_OFF // PROJ_TN, 2 * D_ATTN // PROJ_TN,
                           nb * ctx_len, None).reshape(nb, ctx_len, 2 * D_ATTN)
            kc_off, vc_off = 0, D_ATTN
        else:
            p_c = _project(xc_flat, mods3, ctx_row, w_in_l, 0, D_IN // PROJ_TN, nb * ctx_len,
                           None).reshape(nb, ctx_len, D_IN)
            kc_off, vc_off = K_OFF, V_OFF
        y_attn = _attention(lams[l], gain, p_l, p_c, kc_off, vc_off, ATTN_TQ, lam_init)
        mix_args = (conv_w[l], pool_w_l, pool_scale[l].reshape(1, D_POOL), w_out_l,
                    ln_g[l].reshape(1, D_MODEL), ln_b[l].reshape(1, D_MODEL))
        xl_new = _mix(p_l, y_attn, xl, mods3, latent_row, *mix_args, MIX_TM)
        if not last:
            y_attn_c = _attention_ctx(lams[l], gain, p_c, lam_init)
            xc = _mix(p_c, y_attn_c, xc, mods3, ctx_row, *mix_args, ctx_len)
        xl = xl_new
    return xl
```

```python
import functools
import math

import numpy as np
import jax
import jax.numpy as jnp
from jax import lax
from jax.experimental import pallas as pl
from jax.experimental.pallas import tpu as pltpu

F32 = jnp.float32
BF16 = jnp.bfloat16

D_MODEL = 2048
DEPTH = 2
GRID_W = 64
D_CONV = 512
D_ATTN = 1024
D_POOL = 512
N_HEADS = 8
V_DIM = 128
QK_DIM = 64
ATTN_SCALE = QK_DIM ** -0.5
LOG2_E = math.log2(math.e)
Q_PRESCALE = ATTN_SCALE * LOG2_E
POOL_WINDOWS = (2, 4, 8, 16)
POOL_GROUP = 128
ROPE_THETA = 10000.0
LN_EPS = 1e-5
RMS_EPS = 1e-5
DEEPNORM_ALPHA = (2 * DEPTH) ** 0.25
D_IN = 4 * D_CONV + 4 * D_ATTN + 2 * D_POOL
Q_OFF = 4 * D_CONV
K_OFF = Q_OFF + D_ATTN
V_OFF = K_OFF + D_ATTN
G_OFF = V_OFF + D_ATTN
POOL_OFF = G_OFF + D_ATTN

LANES = 128
SUBLANES = 8
BF16_ROWS = 16
HALO = BF16_ROWS
MOD_ROWS = 8
VMEM_LIMIT = 56 * 1024 * 1024

PROJ_TM = 1024
PROJ_TN = 1024
ATTN_TQ = 256
ATTN_CHUNK = 256
MIX_TM = 512
MIX_SUB = 256
MOD_TN = 1536


def _silu(x):
    return x / (1.0 + jnp.exp(-x))


def _params(*sem):
    return pltpu.CompilerParams(dimension_semantics=sem, vmem_limit_bytes=VMEM_LIMIT)


def _mod_kernel(c_ref, w_ref, b_ref, lamv_ref, mod_ref, lam_ref):
    layer = pl.program_id(0)
    s = _silu(c_ref[...]).astype(BF16)
    w = w_ref[0].astype(BF16)
    mod_ref[0] = jnp.dot(s, w, preferred_element_type=F32) + b_ref[0]
    lv = lamv_ref[0]
    d1 = jnp.sum(lv[0:1] * lv[1:2], axis=-1, keepdims=True)
    d2 = jnp.sum(lv[2:3] * lv[3:4], axis=-1, keepdims=True)
    lf = jnp.zeros((1, LANES), F32) + layer.astype(F32)
    lam_init = 0.8 - 0.6 * jnp.exp(-0.3 * lf)
    lam_ref[0] = jnp.exp(d1) - jnp.exp(d2) + lam_init


def _modulation(cvec, w_mod, b_mod, lamv):
    n_out = w_mod.shape[-1]
    return pl.pallas_call(
        _mod_kernel,
        grid=(DEPTH, n_out // MOD_TN),
        in_specs=[
            pl.BlockSpec((MOD_ROWS, D_MODEL), lambda l, j: (0, 0)),
            pl.BlockSpec((1, D_MODEL, MOD_TN), lambda l, j: (l, 0, j)),
            pl.BlockSpec((1, 1, MOD_TN), lambda l, j: (l, 0, j)),
            pl.BlockSpec((1, 4, QK_DIM), lambda l, j: (l, 0, 0)),
        ],
        out_specs=[
            pl.BlockSpec((1, MOD_ROWS, MOD_TN), lambda l, j: (l, 0, j)),
            pl.BlockSpec((1, 1, LANES), lambda l, j: (l, 0, 0)),
        ],
        out_shape=[
            jax.ShapeDtypeStruct((DEPTH, MOD_ROWS, n_out), F32),
            jax.ShapeDtypeStruct((DEPTH, 1, LANES), F32),
        ],
        compiler_params=_params("arbitrary", "arbitrary"),
        name="modulation",
    )(cvec, w_mod, b_mod.reshape(DEPTH, 1, n_out), lamv)


def _proj_kernel(*refs, rope, q_blk, k_blk):
    if rope:
        x_ref, sh_ref, sc_ref, w_ref, cos_ref, sin_ref, o_ref, h_ref = refs
    else:
        x_ref, sh_ref, sc_ref, w_ref, o_ref, h_ref = refs
    j = pl.program_id(2)

    @pl.when(j == 0)
    def _():
        h_ref[...] = (x_ref[0] * (1.0 + sc_ref[0]) + sh_ref[0]).astype(BF16)

    acc = jnp.dot(h_ref[...], w_ref[...], preferred_element_type=F32)
    o_ref[0] = acc.astype(BF16)

    if rope:
        @pl.when((j == q_blk) | (j == k_blk))
        def _():
            scale = jnp.where(j == q_blk, Q_PRESCALE, 1.0).astype(F32)
            cos = cos_ref[...] * scale
            sin = sin_ref[...] * scale
            for c in range(acc.shape[1] // LANES):
                xc = acc[:, c * LANES:(c + 1) * LANES]
                r = xc * cos + pltpu.roll(xc, LANES // 2, 1) * sin
                o_ref[0, :, c * LANES:(c + 1) * LANES] = r.astype(BF16)


def _project(x, mods3, mod_row, w_bf16, layer, col_blk0, n_col_blks, tm, rope_tables):
    nb, t, _ = x.shape
    tn = PROJ_TN
    rope = rope_tables is not None
    in_specs = [
        pl.BlockSpec((1, tm, D_MODEL), lambda b, i, j: (b, i, 0)),
        pl.BlockSpec((1, 1, D_MODEL), lambda b, i, j: (mod_row(b), 0, 0)),
        pl.BlockSpec((1, 1, D_MODEL), lambda b, i, j: (mod_row(b), 0, 1)),
        pl.BlockSpec((None, D_MODEL, tn), lambda b, i, j: (layer, 0, j + col_blk0)),
    ]
    args = [x, mods3, mods3, w_bf16]
    if rope:
        in_specs += [pl.BlockSpec((tm, LANES), lambda b, i, j: (i, 0))] * 2
        args += list(rope_tables)
    return pl.pallas_call(
        functools.partial(_proj_kernel, rope=rope, q_blk=Q_OFF // tn - col_blk0, k_blk=K_OFF // tn - col_blk0),
        grid=(nb, t // tm, n_col_blks),
        in_specs=in_specs,
        out_specs=pl.BlockSpec((1, tm, tn), lambda b, i, j: (b, i, j)),
        out_shape=jax.ShapeDtypeStruct((nb, t, n_col_blks * tn), BF16),
        scratch_shapes=[pltpu.VMEM((tm, D_MODEL), BF16)],
        compiler_params=_params("arbitrary", "arbitrary", "arbitrary"),
        name="proj_rope" if rope else "proj",
    )(*args)


def _head_layout(w):
    n_layers, d, _ = w.shape
    n_freq = QK_DIM // 4
    qk = w[:, :, Q_OFF:V_OFF].reshape(n_layers, d, 2, N_HEADS, 2, 2, 2, n_freq)
    qk = qk.transpose(0, 1, 2, 3, 6, 4, 5, 7).reshape(n_layers, d, 2 * D_ATTN)
    return lax.dynamic_update_slice(w, qk, (0, 0, Q_OFF))


def _split_halves(q):
    lane = lax.broadcasted_iota(jnp.int32, (1, LANES), 1)
    first = (lane // (QK_DIM // 2)) % 2 == 0
    zero = jnp.zeros_like(q)
    return jnp.concatenate([jnp.where(first, q, zero), jnp.where(first, zero, q)], axis=0)


def _attn_ctx_kernel(lam_ref, gain_ref, q_ref, ag_ref, k_ref, v_ref, o_ref, *, lam_init):
    tq = q_ref.shape[1]
    for h in range(N_HEADS):
        cols = slice(h * V_DIM, (h + 1) * V_DIM)
        qs = _split_halves(q_ref[0, :, cols] * ATTN_SCALE)
        s = lax.dot_general(qs, k_ref[0, :, cols], (((1,), (1,)), ((), ())), preferred_element_type=F32)
        p = jnp.exp(s - s.max(axis=-1, keepdims=True))
        den = p.sum(axis=-1, keepdims=True)
        den1, den2 = den[:tq], den[tq:]
        ratio = lam_ref[:, 0:1] * den1 / den2
        a = (p[:tq] - ratio * p[tq:]).astype(BF16)
        o = jnp.dot(a, v_ref[0, :, cols], preferred_element_type=F32) / den1
        y = o * lax.rsqrt(jnp.mean(o * o, axis=-1, keepdims=True) + RMS_EPS) * (gain_ref[...] * (1.0 - lam_init))
        o_ref[0, :, cols] = (_silu(ag_ref[0, :, cols].astype(F32)) * y).astype(BF16)


def _attention_ctx(lam, gain, p_c, lam_init):
    nb, t, _ = p_c.shape
    col = lambda off: (lambda b: (b, 0, off // D_ATTN))
    return pl.pallas_call(
        functools.partial(_attn_ctx_kernel, lam_init=lam_init),
        grid=(nb,),
        in_specs=[
            pl.BlockSpec((1, LANES), lambda b: (0, 0)),
            pl.BlockSpec((1, V_DIM), lambda b: (0, 0)),
            pl.BlockSpec((1, t, D_ATTN), col(Q_OFF)),
            pl.BlockSpec((1, t, D_ATTN), col(G_OFF)),
            pl.BlockSpec((1, t, D_ATTN), col(K_OFF)),
            pl.BlockSpec((1, t, D_ATTN), col(V_OFF)),
        ],
        out_specs=pl.BlockSpec((1, t, D_ATTN), col(0)),
        out_shape=jax.ShapeDtypeStruct((nb, t, D_ATTN), BF16),
        compiler_params=_params("arbitrary"),
        name="diff_attn_ctx",
    )(lam, gain, p_c, p_c, p_c, p_c)


def _attn_kernel(lam_ref, gain_ref, q_ref, ag_ref, kc_ref, kl_ref, vc_ref, vl_ref, o_ref,
                 s0, s1, m0, m1, p0, p1, l0, l1, *, lam_init):
    step = pl.program_id(0)
    tq = q_ref.shape[1] // 2
    nt = (((1,), (1,)), ((), ()))
    tn = (((0,), (0,)), ((), ()))
    chunks = ([(kc_ref, vc_ref, r) for r in range(0, kc_ref.shape[1], ATTN_CHUNK)]
              + [(kl_ref, vl_ref, r) for r in range(0, kl_ref.shape[1], ATTN_CHUNK)])
    groups = ATTN_CHUNK // SUBLANES

    @pl.when(step == 0)
    def _():
        s1[...] = jnp.zeros_like(s1)
        m1[...] = jnp.zeros_like(m1)
        p0[...] = jnp.zeros_like(p0)
        l0[...] = jnp.ones_like(l0)

    def half(tile_rows, s_w, m_w, s_r, m_r, p_w, l_w, p_r, l_r):
        qs = _split_halves(q_ref[0, tile_rows, :])
        m_prev = m_r[...]
        den = l_r[...]
        den1, den2 = den[:, :tq], den[:, tq:]
        ratio = (lam_ref[:, 0:1] * den1 / den2).astype(BF16)
        m_acc = jnp.full((SUBLANES, 2 * tq), -jnp.inf, F32)
        l_acc = jnp.zeros((SUBLANES, 2 * tq), F32)
        o_t = jnp.zeros((V_DIM, tq), F32)
        scores = {id(r): lax.dot_general(r[0], qs, nt, preferred_element_type=F32) for r in (kc_ref, kl_ref)}
        for c, (k_ref, v_ref, r0) in enumerate(chunks):
            rows = slice(c * ATTN_CHUNK, (c + 1) * ATTN_CHUNK)
            s = scores[id(k_ref)][r0:r0 + ATTN_CHUNK]
            s_w[rows, :] = s
            pace = jnp.where(l_acc < 0.0, 1.0, 0.0)
            m_acc = jnp.maximum(m_acc + pace, s.reshape(groups, SUBLANES, 2 * tq).max(axis=0))
            p = jnp.exp2(s_r[rows, :] - m_prev)
            l_acc = l_acc + p.reshape(groups, SUBLANES, 2 * tq).sum(axis=0)
            p_w[rows, :] = p.astype(BF16)
            pr = p_r[rows, :]
            a = pr[:, :tq] - ratio * pr[:, tq:]
            o_t = o_t + lax.dot_general(v_ref[0, r0:r0 + ATTN_CHUNK, :], a, tn, preferred_element_type=F32)
        m_w[...] = m_acc.max(axis=0, keepdims=True)
        l_w[...] = l_acc.sum(axis=0, keepdims=True)
        o = (o_t * (1.0 / den1)).T
        y = o * lax.rsqrt(jnp.mean(o * o, axis=-1, keepdims=True) + RMS_EPS) * (gain_ref[...] * (1.0 - lam_init))
        o_ref[0, tile_rows, :] = (_silu(ag_ref[0, tile_rows, :].astype(F32)) * y).astype(BF16)

    half(slice(0, tq), s0, m0, s1, m1, p1, l1, p0, l0)
    half(slice(tq, 2 * tq), s1, m1, s0, m0, p0, l0, p1, l1)


def _attention(lam, gain, p_l, p_c, kc_off, vc_off, tq, lam_init):
    nb, seq, _ = p_l.shape
    n_ctx = p_c.shape[1]
    assert n_ctx % ATTN_CHUNK == 0 and seq % ATTN_CHUNK == 0 and seq % (2 * tq) == 0
    n_keys = n_ctx + seq
    pairs_per_head = seq // (2 * tq)
    n_pairs = nb * N_HEADS * pairs_per_head

    def pair(k):
        k = jnp.clip(k, 0, n_pairs - 1)
        return k // (N_HEADS * pairs_per_head), (k // pairs_per_head) % N_HEADS, k % pairs_per_head

    def tile_map(off, lag):
        def f(k):
            b, h, i = pair(k - lag)
            return b, i, off // V_DIM + h
        return f

    def head_map(off, lag):
        def f(k):
            b, h, _ = pair(k - lag)
            return b, 0, off // V_DIM + h
        return f

    return pl.pallas_call(
        functools.partial(_attn_kernel, lam_init=lam_init),
        grid=(n_pairs + 1,),
        in_specs=[
            pl.BlockSpec((1, LANES), lambda k: (0, 0)),
            pl.BlockSpec((1, V_DIM), lambda k: (0, 0)),
            pl.BlockSpec((1, 2 * tq, V_DIM), tile_map(Q_OFF, 0)),
            pl.BlockSpec((1, 2 * tq, V_DIM), tile_map(G_OFF, 1)),
            pl.BlockSpec((1, n_ctx, V_DIM), head_map(kc_off, 0)),
            pl.BlockSpec((1, seq, V_DIM), head_map(K_OFF, 0)),
            pl.BlockSpec((1, n_ctx, V_DIM), head_map(vc_off, 1)),
            pl.BlockSpec((1, seq, V_DIM), head_map(V_OFF, 1)),
        ],
        out_specs=pl.BlockSpec((1, 2 * tq, V_DIM), tile_map(0, 1)),
        out_shape=jax.ShapeDtypeStruct((nb, seq, D_ATTN), BF16),
        scratch_shapes=[
            pltpu.VMEM((n_keys, 2 * tq), F32), pltpu.VMEM((n_keys, 2 * tq), F32),
            pltpu.VMEM((1, 2 * tq), F32), pltpu.VMEM((1, 2 * tq), F32),
            pltpu.VMEM((n_keys, 2 * tq), BF16), pltpu.VMEM((n_keys, 2 * tq), BF16),
            pltpu.VMEM((1, 2 * tq), F32), pltpu.VMEM((1, 2 * tq), F32),
        ],
        compiler_params=_params("arbitrary"),
        name="diff_attn",
    )(lam, gain, p_l, p_l, p_c, p_l, p_c, p_l)


def _mix_kernel(pc_ref, cup_ref, ccp_ref, cun_ref, ccn_ref, pp_ref, pup_ref, pun_ref, ya_ref, x_ref, g_ref,
                cw_ref, pw_ref, ps_ref, wo_ref, lg_ref, lb_ref, o_ref, mix_ref, *, seq_len):
    i = pl.program_id(1)
    tm = x_ref.shape[1]
    n_tiles = seq_len // tm
    has_prev = i > 0
    has_next = i < n_tiles - 1
    row = lax.broadcasted_iota(jnp.int32, (tm, 1), 0)

    cu = pc_ref[0, :, 0:D_CONV].astype(F32)
    cb = pc_ref[0, :, D_CONV:2 * D_CONV].astype(F32)
    cc = pc_ref[0, :, 2 * D_CONV:3 * D_CONV].astype(F32)
    cg = pc_ref[0, :, 3 * D_CONV:4 * D_CONV].astype(F32)
    v = cc * cu
    v_before = (ccp_ref[0].astype(F32) * cup_ref[0].astype(F32))[HALO - 1:HALO]
    v_after = (ccn_ref[0].astype(F32) * cun_ref[0].astype(F32))[0:1]
    v_before = jnp.where(has_prev, v_before, 0.0)
    v_after = jnp.where(has_next, v_after, 0.0)
    v_m1 = jnp.where(row == 0, v_before, pltpu.roll(v, 1, 0))
    v_p1 = jnp.where(row == tm - 1, v_after, pltpu.roll(v, tm - 1, 0))
    cw = cw_ref[...]
    y_conv = _silu(cg) * (cb * (v_m1 * cw[0:1] + v * cw[1:2] + v_p1 * cw[2:3]))
    mix_ref[:, 0:D_CONV] = y_conv.astype(BF16)

    mix_ref[:, D_CONV:D_CONV + D_ATTN] = ya_ref[0]

    u = pp_ref[0, :, 0:D_POOL].astype(F32)
    pg = pp_ref[0, :, D_POOL:2 * D_POOL].astype(F32)
    u_before = jnp.where(has_prev, pup_ref[0].astype(F32), 0.0)
    u_after = jnp.where(has_next, pun_ref[0].astype(F32), 0.0)
    u_ext = jnp.concatenate([u_before, u, u_after], axis=0)
    n_ext = tm + 2 * HALO
    t = i * tm + row
    for gi, w in enumerate(POOL_WINDOWS):
        sl = slice(gi * POOL_GROUP, (gi + 1) * POOL_GROUP)
        s = u_ext[:, sl]
        k = 1
        while k < w:
            s = s + pltpu.roll(s, n_ext - k, 0)
            k *= 2
        win = pltpu.roll(s, w // 2, 0)[HALO:HALO + tm]
        lo = jnp.maximum(t - w // 2, 0)
        hi = jnp.minimum(t + w - w // 2, seq_len)
        d = (win / (hi - lo).astype(F32) - u[:, sl]).astype(BF16)
        yg = jnp.dot(d, pw_ref[gi], preferred_element_type=F32)
        y_pool = _silu(pg[:, sl]) * (yg * ps_ref[:, sl])
        mix_ref[:, D_CONV + D_ATTN + gi * POOL_GROUP:D_CONV + D_ATTN + (gi + 1) * POOL_GROUP] = y_pool.astype(BF16)

    pace = None
    for r in range(0, tm, MIX_SUB):
        rs = slice(r, r + MIX_SUB)
        lhs = mix_ref[rs, :]
        if pace is not None:
            lhs = lhs + pace
        out = jnp.dot(lhs, wo_ref[...], preferred_element_type=F32)
        pace = jnp.where(out[0:1, :] < -jnp.inf, 1.0, 0.0).astype(BF16)
        z = DEEPNORM_ALPHA * x_ref[0, rs, :] + g_ref[0] * out
        mu = jnp.mean(z, axis=-1, keepdims=True)
        zc = z - mu
        var = jnp.mean(zc * zc, axis=-1, keepdims=True)
        o_ref[0, rs, :] = zc * lax.rsqrt(var + LN_EPS) * lg_ref[...] + lb_ref[...]


def _mix(p, y_attn, x, mods3, mod_row, layer, conv_w, pool_w_bf16, pool_scale, w_out_bf16, ln_g, ln_b, tm):
    nb, t, _ = x.shape
    hb = tm // HALO
    last = t // HALO - 1
    prev = lambda i: jnp.maximum(i * hb - 1, 0)
    nxt = lambda i: jnp.minimum((i + 1) * hb, last)
    cc_blk = 2
    const2 = lambda b, i: (0, 0)
    in_specs = [
        pl.BlockSpec((1, tm, 4 * D_CONV), lambda b, i: (b, i, 0)),
        pl.BlockSpec((1, HALO, D_CONV), lambda b, i: (b, prev(i), 0)),
        pl.BlockSpec((1, HALO, D_CONV), lambda b, i: (b, prev(i), cc_blk)),
        pl.BlockSpec((1, HALO, D_CONV), lambda b, i: (b, nxt(i), 0)),
        pl.BlockSpec((1, HALO, D_CONV), lambda b, i: (b, nxt(i), cc_blk)),
        pl.BlockSpec((1, tm, 2 * D_POOL), lambda b, i: (b, i, POOL_OFF // (2 * D_POOL))),
        pl.BlockSpec((1, HALO, D_POOL), lambda b, i: (b, prev(i), POOL_OFF // D_POOL)),
        pl.BlockSpec((1, HALO, D_POOL), lambda b, i: (b, nxt(i), POOL_OFF // D_POOL)),
        pl.BlockSpec((1, tm, D_ATTN), lambda b, i: (b, i, 0)),
        pl.BlockSpec((1, tm, D_MODEL), lambda b, i: (b, i, 0)),
        pl.BlockSpec((1, 1, D_MODEL), lambda b, i: (mod_row(b), 0, 2)),
        pl.BlockSpec(conv_w.shape, const2),
        pl.BlockSpec((None,) + pool_w_bf16.shape[1:], lambda b, i: (layer, 0, 0, 0)),
        pl.BlockSpec(pool_scale.shape, const2),
        pl.BlockSpec((None,) + w_out_bf16.shape[1:], lambda b, i: (layer, 0, 0), pipeline_mode=pl.Buffered(1)),
        pl.BlockSpec(ln_g.shape, const2),
        pl.BlockSpec(ln_b.shape, const2),
    ]
    return pl.pallas_call(
        functools.partial(_mix_kernel, seq_len=t),
        grid=(nb, t // tm),
        in_specs=in_specs,
        out_specs=pl.BlockSpec((1, tm, D_MODEL), lambda b, i: (b, i, 0)),
        out_shape=jax.ShapeDtypeStruct((nb, t, D_MODEL), F32),
        scratch_shapes=[pltpu.VMEM((tm, D_MODEL), BF16)],
        compiler_params=_params("arbitrary", "arbitrary"),
        name="mix_out_ln",
    )(p, p, p, p, p, p, p, p, y_attn, x, mods3, conv_w, pool_w_bf16, pool_scale, w_out_bf16, ln_g, ln_b)


def _rope_tables(seq_len):
    n = QK_DIM // 4
    inv = ROPE_THETA ** (-np.arange(n, dtype=np.float64) / n)
    tok = np.arange(seq_len)
    pos = np.stack([tok // GRID_W, tok % GRID_W], axis=1).astype(np.float64)
    m = np.arange(LANES)
    ang = pos[:, (m // n) % 2] * inv[m % n][None, :]
    second = (m >= LANES // 2)[None, :]
    cos, sin = np.cos(ang), np.sin(ang)
    return jnp.asarray(cos, dtype=F32), jnp.asarray(np.where(second, sin, -sin), dtype=F32)


def kernel(x, c, ctx, c_ctx, w_mod, b_mod, w_in, conv_w, lam_q1, lam_k1, lam_q2, lam_k2, subln_g, pool_w,
           pool_scale, w_out, ln_g, ln_b):
    nb, seq, _ = x.shape
    ctx_len = ctx.shape[1]
    assert nb + 1 <= MOD_ROWS

    cvec = jnp.concatenate([c, c_ctx[None, :], jnp.zeros((MOD_ROWS - nb - 1, D_MODEL), F32)], axis=0)
    lamv = jnp.stack([lam_q1, lam_k1, lam_q2, lam_k2], axis=1).astype(F32)
    mods, lams = _modulation(cvec, w_mod, b_mod, lamv)
    tables = _rope_tables(seq)
    latent_row = lambda b: b
    ctx_row = lambda b: nb

    w_in_bf = _head_layout(w_in.astype(BF16))
    w_out_bf = w_out.astype(BF16)
    pool_w_bf = pool_w.astype(BF16)

    xl, xc = x, ctx
    for l in range(DEPTH):
        last = l == DEPTH - 1
        lam_init = 0.8 - 0.6 * math.exp(-0.3 * l)
        mods3 = mods[l].reshape(MOD_ROWS, 1, 3 * D_MODEL)
        gain = subln_g[l].reshape(1, V_DIM)
        p_l = _project(xl, mods3, latent_row, w_in_bf, l, 0, D_IN // PROJ_TN, PROJ_TM, tables)
        xc_flat = xc.reshape(1, nb * ctx_len, D_MODEL)
        if last:
            p_c = _project(xc_flat, mods3, ctx_row, w_in_bf, l, K_OFF // PROJ_TN, 2 * D_ATTN // PROJ_TN,
                           nb * ctx_len, None).reshape(nb, ctx_len, 2 * D_ATTN)
            kc_off, vc_off = 0, D_ATTN
        else:
            p_c = _project(xc_flat, mods3, ctx_row, w_in_bf, l, 0, D_IN // PROJ_TN, nb * ctx_len,
                           None).reshape(nb, ctx_len, D_IN)
            kc_off, vc_off = K_OFF, V_OFF
        y_attn = _attention(lams[l], gain, p_l, p_c, kc_off, vc_off, ATTN_TQ, lam_init)
        mix_args = (l, conv_w[l], pool_w_bf, pool_scale[l].reshape(1, D_POOL), w_out_bf,
                    ln_g[l].reshape(1, D_MODEL), ln_b[l].reshape(1, D_MODEL))
        xl_new = _mix(p_l, y_attn, xl, mods3, latent_row, *mix_args, MIX_TM)
        if not last:
            y_attn_c = _attention_ctx(lams[l], gain, p_c, lam_init)
            xc = _mix(p_c, y_attn_c, xc, mods3, ctx_row, *mix_args, ctx_len)
        xl = xl_new
    return xl
```

```python
import functools
import math

import numpy as np
import jax
import jax.numpy as jnp
from jax import lax
from jax.experimental import pallas as pl
from jax.experimental.pallas import tpu as pltpu

F32 = jnp.float32
BF16 = jnp.bfloat16

D_MODEL = 2048
DEPTH = 2
GRID_W = 64
D_CONV = 512
D_ATTN = 1024
D_POOL = 512
N_HEADS = 8
V_DIM = 128
QK_DIM = 64
ATTN_SCALE = QK_DIM ** -0.5
LOG2_E = math.log2(math.e)
Q_PRESCALE = ATTN_SCALE * LOG2_E
POOL_WINDOWS = (2, 4, 8, 16)
POOL_GROUP = 128
ROPE_THETA = 10000.0
LN_EPS = 1e-5
RMS_EPS = 1e-5
DEEPNORM_ALPHA = (2 * DEPTH) ** 0.25
D_IN = 4 * D_CONV + 4 * D_ATTN + 2 * D_POOL
Q_OFF = 4 * D_CONV
K_OFF = Q_OFF + D_ATTN
V_OFF = K_OFF + D_ATTN
G_OFF = V_OFF + D_ATTN
POOL_OFF = G_OFF + D_ATTN

LANES = 128
SUBLANES = 8
BF16_ROWS = 16
HALO = BF16_ROWS
MOD_ROWS = 8
VMEM_LIMIT = 56 * 1024 * 1024

PROJ_TM = 1024
PROJ_TN = 1024
ATTN_TQ = 512
ATTN_CHUNK = 256
MIX_TM = 512
MIX_SUB = 256
MOD_TN = 1536


def _silu(x):
    return x / (1.0 + jnp.exp(-x))


def _params(*sem):
    return pltpu.CompilerParams(dimension_semantics=sem, vmem_limit_bytes=VMEM_LIMIT)


def _mod_kernel(c_ref, w_ref, b_ref, lamv_ref, mod_ref, lam_ref):
    layer = pl.program_id(0)
    s = _silu(c_ref[...]).astype(BF16)
    w = w_ref[0].astype(BF16)
    mod_ref[0] = jnp.dot(s, w, preferred_element_type=F32) + b_ref[0]
    lv = lamv_ref[0]
    d1 = jnp.sum(lv[0:1] * lv[1:2], axis=-1, keepdims=True)
    d2 = jnp.sum(lv[2:3] * lv[3:4], axis=-1, keepdims=True)
    lf = jnp.zeros((1, LANES), F32) + layer.astype(F32)
    lam_init = 0.8 - 0.6 * jnp.exp(-0.3 * lf)
    lam_ref[0] = jnp.exp(d1) - jnp.exp(d2) + lam_init


def _modulation(cvec, w_mod, b_mod, lamv):
    n_out = w_mod.shape[-1]
    return pl.pallas_call(
        _mod_kernel,
        grid=(DEPTH, n_out // MOD_TN),
        in_specs=[
            pl.BlockSpec((MOD_ROWS, D_MODEL), lambda l, j: (0, 0)),
            pl.BlockSpec((1, D_MODEL, MOD_TN), lambda l, j: (l, 0, j)),
            pl.BlockSpec((1, 1, MOD_TN), lambda l, j: (l, 0, j)),
            pl.BlockSpec((1, 4, QK_DIM), lambda l, j: (l, 0, 0)),
        ],
        out_specs=[
            pl.BlockSpec((1, MOD_ROWS, MOD_TN), lambda l, j: (l, 0, j)),
            pl.BlockSpec((1, 1, LANES), lambda l, j: (l, 0, 0)),
        ],
        out_shape=[
            jax.ShapeDtypeStruct((DEPTH, MOD_ROWS, n_out), F32),
            jax.ShapeDtypeStruct((DEPTH, 1, LANES), F32),
        ],
        compiler_params=_params("arbitrary", "arbitrary"),
        name="modulation",
    )(cvec, w_mod, b_mod.reshape(DEPTH, 1, n_out), lamv)


def _proj_kernel(*refs, rope, q_blk, k_blk):
    if rope:
        x_ref, sh_ref, sc_ref, w_ref, cos_ref, sin_ref, o_ref, h_ref = refs
    else:
        x_ref, sh_ref, sc_ref, w_ref, o_ref, h_ref = refs
    j = pl.program_id(2)

    @pl.when(j == 0)
    def _():
        h_ref[...] = (x_ref[0] * (1.0 + sc_ref[0]) + sh_ref[0]).astype(BF16)

    acc = jnp.dot(h_ref[...], w_ref[...], preferred_element_type=F32)
    o_ref[0] = acc.astype(BF16)

    if rope:
        @pl.when((j == q_blk) | (j == k_blk))
        def _():
            scale = jnp.where(j == q_blk, Q_PRESCALE, 1.0).astype(F32)
            cos = cos_ref[...] * scale
            sin = sin_ref[...] * scale
            for c in range(acc.shape[1] // LANES):
                xc = acc[:, c * LANES:(c + 1) * LANES]
                r = xc * cos + pltpu.roll(xc, LANES // 2, 1) * sin
                o_ref[0, :, c * LANES:(c + 1) * LANES] = r.astype(BF16)


def _project(x, mods3, mod_row, w_bf16, layer, col_blk0, n_col_blks, tm, rope_tables):
    nb, t, _ = x.shape
    tn = PROJ_TN
    rope = rope_tables is not None
    in_specs = [
        pl.BlockSpec((1, tm, D_MODEL), lambda b, i, j: (b, i, 0)),
        pl.BlockSpec((1, 1, D_MODEL), lambda b, i, j: (mod_row(b), 0, 0)),
        pl.BlockSpec((1, 1, D_MODEL), lambda b, i, j: (mod_row(b), 0, 1)),
        pl.BlockSpec((None, D_MODEL, tn), lambda b, i, j: (layer, 0, j + col_blk0)),
    ]
    args = [x, mods3, mods3, w_bf16]
    if rope:
        in_specs += [pl.BlockSpec((tm, LANES), lambda b, i, j: (i, 0))] * 2
        args += list(rope_tables)
    return pl.pallas_call(
        functools.partial(_proj_kernel, rope=rope, q_blk=Q_OFF // tn - col_blk0, k_blk=K_OFF // tn - col_blk0),
        grid=(nb, t // tm, n_col_blks),
        in_specs=in_specs,
        out_specs=pl.BlockSpec((1, tm, tn), lambda b, i, j: (b, i, j)),
        out_shape=jax.ShapeDtypeStruct((nb, t, n_col_blks * tn), BF16),
        scratch_shapes=[pltpu.VMEM((tm, D_MODEL), BF16)],
        compiler_params=_params("arbitrary", "arbitrary", "arbitrary"),
        name="proj_rope" if rope else "proj",
    )(*args)


def _head_layout(w):
    n_layers, d, _ = w.shape
    n_freq = QK_DIM // 4
    qk = w[:, :, Q_OFF:V_OFF].reshape(n_layers, d, 2, N_HEADS, 2, 2, 2, n_freq)
    qk = qk.transpose(0, 1, 2, 3, 6, 4, 5, 7).reshape(n_layers, d, 2 * D_ATTN)
    return lax.dynamic_update_slice(w, qk, (0, 0, Q_OFF))


def _split_halves(q):
    lane = lax.broadcasted_iota(jnp.int32, (1, LANES), 1)
    first = (lane // (QK_DIM // 2)) % 2 == 0
    zero = jnp.zeros_like(q)
    return jnp.concatenate([jnp.where(first, q, zero), jnp.where(first, zero, q)], axis=0)


def _attn_ctx_kernel(lam_ref, gain_ref, q_ref, ag_ref, k_ref, v_ref, o_ref, *, lam_init):
    tq = q_ref.shape[1]
    for h in range(N_HEADS):
        cols = slice(h * V_DIM, (h + 1) * V_DIM)
        qs = _split_halves(q_ref[0, :, cols] * ATTN_SCALE)
        s = lax.dot_general(qs, k_ref[0, :, cols], (((1,), (1,)), ((), ())), preferred_element_type=F32)
        p = jnp.exp(s - s.max(axis=-1, keepdims=True))
        den = p.sum(axis=-1, keepdims=True)
        den1, den2 = den[:tq], den[tq:]
        ratio = lam_ref[:, 0:1] * den1 / den2
        a = (p[:tq] - ratio * p[tq:]).astype(BF16)
        o = jnp.dot(a, v_ref[0, :, cols], preferred_element_type=F32) / den1
        y = o * lax.rsqrt(jnp.mean(o * o, axis=-1, keepdims=True) + RMS_EPS) * (gain_ref[...] * (1.0 - lam_init))
        o_ref[0, :, cols] = (_silu(ag_ref[0, :, cols].astype(F32)) * y).astype(BF16)


def _attention_ctx(lam, gain, p_c, lam_init):
    nb, t, _ = p_c.shape
    col = lambda off: (lambda b: (b, 0, off // D_ATTN))
    return pl.pallas_call(
        functools.partial(_attn_ctx_kernel, lam_init=lam_init),
        grid=(nb,),
        in_specs=[
            pl.BlockSpec((1, LANES), lambda b: (0, 0)),
            pl.BlockSpec((1, V_DIM), lambda b: (0, 0)),
            pl.BlockSpec((1, t, D_ATTN), col(Q_OFF)),
            pl.BlockSpec((1, t, D_ATTN), col(G_OFF)),
            pl.BlockSpec((1, t, D_ATTN), col(K_OFF)),
            pl.BlockSpec((1, t, D_ATTN), col(V_OFF)),
        ],
        out_specs=pl.BlockSpec((1, t, D_ATTN), col(0)),
        out_shape=jax.ShapeDtypeStruct((nb, t, D_ATTN), BF16),
        compiler_params=_params("arbitrary"),
        name="diff_attn_ctx",
    )(lam, gain, p_c, p_c, p_c, p_c)


def _attn_kernel(lam_ref, gain_ref, q_ref, ag_ref, kc_ref, kl_ref, vc_ref, vl_ref, o_ref,
                 s_ref, m_ref, p_ref, l_ref, *, lam_init):
    step = pl.program_id(0)
    tq = q_ref.shape[1] // 2
    nt = (((1,), (1,)), ((), ()))
    tn = (((0,), (0,)), ((), ()))
    chunks = ([(kc_ref, vc_ref, r) for r in range(0, kc_ref.shape[1], ATTN_CHUNK)]
              + [(kl_ref, vl_ref, r) for r in range(0, kl_ref.shape[1], ATTN_CHUNK)])
    groups = ATTN_CHUNK // SUBLANES

    @pl.when(step == 0)
    def _():
        s_ref[...] = jnp.zeros_like(s_ref)
        m_ref[...] = jnp.zeros_like(m_ref)
        p_ref[...] = jnp.zeros_like(p_ref)
        l_ref[...] = jnp.ones_like(l_ref)

    def half(tile_rows):
        qs = _split_halves(q_ref[0, tile_rows, :])
        m_prev = m_ref[...]
        den = l_ref[...]
        den1, den2 = den[:, :tq], den[:, tq:]
        ratio = (lam_ref[:, 0:1] * den1 / den2).astype(BF16)
        m_acc = jnp.full((SUBLANES, 2 * tq), -jnp.inf, F32)
        l_acc = jnp.zeros((SUBLANES, 2 * tq), F32)
        o_t = jnp.zeros((V_DIM, tq), F32)
        for c, (k_ref, v_ref, r0) in enumerate(chunks):
            rows = slice(c * ATTN_CHUNK, (c + 1) * ATTN_CHUNK)
            x = s_ref[rows, :]
            pr = p_ref[rows, :]
            s = lax.dot_general(k_ref[0, r0:r0 + ATTN_CHUNK, :], qs, nt, preferred_element_type=F32)
            s_ref[rows, :] = s
            m_acc = jnp.maximum(m_acc, s.reshape(groups, SUBLANES, 2 * tq).max(axis=0))
            p = jnp.exp2(x - m_prev)
            l_acc = l_acc + p.reshape(groups, SUBLANES, 2 * tq).sum(axis=0)
            p_ref[rows, :] = p.astype(BF16)
            a = pr[:, :tq] - ratio * pr[:, tq:]
            o_t = o_t + lax.dot_general(v_ref[0, r0:r0 + ATTN_CHUNK, :], a, tn, preferred_element_type=F32)
        m_ref[...] = m_acc.max(axis=0, keepdims=True)
        l_ref[...] = l_acc.sum(axis=0, keepdims=True)
        o = (o_t * (1.0 / den1)).T
        y = o * lax.rsqrt(jnp.mean(o * o, axis=-1, keepdims=True) + RMS_EPS) * (gain_ref[...] * (1.0 - lam_init))
        o_ref[0, tile_rows, :] = (_silu(ag_ref[0, tile_rows, :].astype(F32)) * y).astype(BF16)

    half(slice(0, tq))
    half(slice(tq, 2 * tq))


def _attention(lam, gain, p_l, p_c, kc_off, vc_off, tq, lam_init):
    nb, seq, _ = p_l.shape
    n_ctx = p_c.shape[1]
    assert n_ctx % ATTN_CHUNK == 0 and seq % ATTN_CHUNK == 0 and seq % (2 * tq) == 0
    n_keys = n_ctx + seq
    pairs_per_head = seq // (2 * tq)
    n_pairs = nb * N_HEADS * pairs_per_head

    def pair(k):
        k = jnp.clip(k, 0, n_pairs - 1)
        return k // (N_HEADS * pairs_per_head), (k // pairs_per_head) % N_HEADS, k % pairs_per_head

    def tile_map(off, lag):
        def f(k):
            b, h, i = pair(k - lag)
            return b, i, off // V_DIM + h
        return f

    def head_map(off, lag):
        def f(k):
            b, h, _ = pair(k - lag)
            return b, 0, off // V_DIM + h
        return f

    return pl.pallas_call(
        functools.partial(_attn_kernel, lam_init=lam_init),
        grid=(n_pairs + 1,),
        in_specs=[
            pl.BlockSpec((1, LANES), lambda k: (0, 0)),
            pl.BlockSpec((1, V_DIM), lambda k: (0, 0)),
            pl.BlockSpec((1, 2 * tq, V_DIM), tile_map(Q_OFF, 0)),
            pl.BlockSpec((1, 2 * tq, V_DIM), tile_map(G_OFF, 1)),
            pl.BlockSpec((1, n_ctx, V_DIM), head_map(kc_off, 0)),
            pl.BlockSpec((1, seq, V_DIM), head_map(K_OFF, 0)),
            pl.BlockSpec((1, n_ctx, V_DIM), head_map(vc_off, 1)),
            pl.BlockSpec((1, seq, V_DIM), head_map(V_OFF, 1)),
        ],
        out_specs=pl.BlockSpec((1, 2 * tq, V_DIM), tile_map(0, 1)),
        out_shape=jax.ShapeDtypeStruct((nb, seq, D_ATTN), BF16),
        scratch_shapes=[
            pltpu.VMEM((n_keys, 2 * tq), F32), pltpu.VMEM((1, 2 * tq), F32),
            pltpu.VMEM((n_keys, 2 * tq), BF16), pltpu.VMEM((1, 2 * tq), F32),
        ],
        compiler_params=_params("arbitrary"),
        name="diff_attn",
    )(lam, gain, p_l, p_l, p_c, p_l, p_c, p_l)


def _mix_kernel(pc_ref, cup_ref, ccp_ref, cun_ref, ccn_ref, pp_ref, pup_ref, pun_ref, ya_ref, x_ref, g_ref,
                cw_ref, pw_ref, ps_ref, wo_ref, lg_ref, lb_ref, o_ref, mix_ref, *, seq_len):
    i = pl.program_id(1)
    tm = x_ref.shape[1]
    n_tiles = seq_len // tm
    has_prev = i > 0
    has_next = i < n_tiles - 1
    row = lax.broadcasted_iota(jnp.int32, (tm, 1), 0)

    cu = pc_ref[0, :, 0:D_CONV].astype(F32)
    cb = pc_ref[0, :, D_CONV:2 * D_CONV].astype(F32)
    cc = pc_ref[0, :, 2 * D_CONV:3 * D_CONV].astype(F32)
    cg = pc_ref[0, :, 3 * D_CONV:4 * D_CONV].astype(F32)
    v = cc * cu
    v_before = (ccp_ref[0].astype(F32) * cup_ref[0].astype(F32))[HALO - 1:HALO]
    v_after = (ccn_ref[0].astype(F32) * cun_ref[0].astype(F32))[0:1]
    v_before = jnp.where(has_prev, v_before, 0.0)
    v_after = jnp.where(has_next, v_after, 0.0)
    v_m1 = jnp.where(row == 0, v_before, pltpu.roll(v, 1, 0))
    v_p1 = jnp.where(row == tm - 1, v_after, pltpu.roll(v, tm - 1, 0))
    cw = cw_ref[...]
    y_conv = _silu(cg) * (cb * (v_m1 * cw[0:1] + v * cw[1:2] + v_p1 * cw[2:3]))
    mix_ref[:, 0:D_CONV] = y_conv.astype(BF16)

    mix_ref[:, D_CONV:D_CONV + D_ATTN] = ya_ref[0]

    u = pp_ref[0, :, 0:D_POOL].astype(F32)
    pg = pp_ref[0, :, D_POOL:2 * D_POOL].astype(F32)
    u_before = jnp.where(has_prev, pup_ref[0].astype(F32), 0.0)
    u_after = jnp.where(has_next, pun_ref[0].astype(F32), 0.0)
    u_ext = jnp.concatenate([u_before, u, u_after], axis=0)
    n_ext = tm + 2 * HALO
    t = i * tm + row
    for gi, w in enumerate(POOL_WINDOWS):
        sl = slice(gi * POOL_GROUP, (gi + 1) * POOL_GROUP)
        s = u_ext[:, sl]
        k = 1
        while k < w:
            s = s + pltpu.roll(s, n_ext - k, 0)
            k *= 2
        win = pltpu.roll(s, w // 2, 0)[HALO:HALO + tm]
        lo = jnp.maximum(t - w // 2, 0)
        hi = jnp.minimum(t + w - w // 2, seq_len)
        d = (win / (hi - lo).astype(F32) - u[:, sl]).astype(BF16)
        yg = jnp.dot(d, pw_ref[gi], preferred_element_type=F32)
        y_pool = _silu(pg[:, sl]) * (yg * ps_ref[:, sl])
        mix_ref[:, D_CONV + D_ATTN + gi * POOL_GROUP:D_CONV + D_ATTN + (gi + 1) * POOL_GROUP] = y_pool.astype(BF16)

    pace = None
    for r in range(0, tm, MIX_SUB):
        rs = slice(r, r + MIX_SUB)
        lhs = mix_ref[rs, :]
        if pace is not None:
            lhs = lhs + pace
        out = jnp.dot(lhs, wo_ref[...], preferred_element_type=F32)
        pace = jnp.where(out[0:1, :] < -jnp.inf, 1.0, 0.0).astype(BF16)
        z = DEEPNORM_ALPHA * x_ref[0, rs, :] + g_ref[0] * out
        mu = jnp.mean(z, axis=-1, keepdims=True)
        zc = z - mu
        var = jnp.mean(zc * zc, axis=-1, keepdims=True)
        o_ref[0, rs, :] = zc * lax.rsqrt(var + LN_EPS) * lg_ref[...] + lb_ref[...]


def _mix(p, y_attn, x, mods3, mod_row, layer, conv_w, pool_w_bf16, pool_scale, w_out_bf16, ln_g, ln_b, tm):
    nb, t, _ = x.shape
    hb = tm // HALO
    last = t // HALO - 1
    prev = lambda i: jnp.maximum(i * hb - 1, 0)
    nxt = lambda i: jnp.minimum((i + 1) * hb, last)
    cc_blk = 2
    const2 = lambda b, i: (0, 0)
    in_specs = [
        pl.BlockSpec((1, tm, 4 * D_CONV), lambda b, i: (b, i, 0)),
        pl.BlockSpec((1, HALO, D_CONV), lambda b, i: (b, prev(i), 0)),
        pl.BlockSpec((1, HALO, D_CONV), lambda b, i: (b, prev(i), cc_blk)),
        pl.BlockSpec((1, HALO, D_CONV), lambda b, i: (b, nxt(i), 0)),
        pl.BlockSpec((1, HALO, D_CONV), lambda b, i: (b, nxt(i), cc_blk)),
        pl.BlockSpec((1, tm, 2 * D_POOL), lambda b, i: (b, i, POOL_OFF // (2 * D_POOL))),
        pl.BlockSpec((1, HALO, D_POOL), lambda b, i: (b, prev(i), POOL_OFF // D_POOL)),
        pl.BlockSpec((1, HALO, D_POOL), lambda b, i: (b, nxt(i), POOL_OFF // D_POOL)),
        pl.BlockSpec((1, tm, D_ATTN), lambda b, i: (b, i, 0)),
        pl.BlockSpec((1, tm, D_MODEL), lambda b, i: (b, i, 0)),
        pl.BlockSpec((1, 1, D_MODEL), lambda b, i: (mod_row(b), 0, 2)),
        pl.BlockSpec(conv_w.shape, const2),
        pl.BlockSpec((None,) + pool_w_bf16.shape[1:], lambda b, i: (layer, 0, 0, 0)),
        pl.BlockSpec(pool_scale.shape, const2),
        pl.BlockSpec((None,) + w_out_bf16.shape[1:], lambda b, i: (layer, 0, 0), pipeline_mode=pl.Buffered(1)),
        pl.BlockSpec(ln_g.shape, const2),
        pl.BlockSpec(ln_b.shape, const2),
    ]
    return pl.pallas_call(
        functools.partial(_mix_kernel, seq_len=t),
        grid=(nb, t // tm),
        in_specs=in_specs,
        out_specs=pl.BlockSpec((1, tm, D_MODEL), lambda b, i: (b, i, 0)),
        out_shape=jax.ShapeDtypeStruct((nb, t, D_MODEL), F32),
        scratch_shapes=[pltpu.VMEM((tm, D_MODEL), BF16)],
        compiler_params=_params("arbitrary", "arbitrary"),
        name="mix_out_ln",
    )(p, p, p, p, p, p, p, p, y_attn, x, mods3, conv_w, pool_w_bf16, pool_scale, w_out_bf16, ln_g, ln_b)


def _rope_tables(seq_len):
    n = QK_DIM // 4
    inv = ROPE_THETA ** (-np.arange(n, dtype=np.float64) / n)
    tok = np.arange(seq_len)
    pos = np.stack([tok // GRID_W, tok % GRID_W], axis=1).astype(np.float64)
    m = np.arange(LANES)
    ang = pos[:, (m // n) % 2] * inv[m % n][None, :]
    second = (m >= LANES // 2)[None, :]
    cos, sin = np.cos(ang), np.sin(ang)
    return jnp.asarray(cos, dtype=F32), jnp.asarray(np.where(second, sin, -sin), dtype=F32)


def kernel(x, c, ctx, c_ctx, w_mod, b_mod, w_in, conv_w, lam_q1, lam_k1, lam_q2, lam_k2, subln_g, pool_w,
           pool_scale, w_out, ln_g, ln_b):
    nb, seq, _ = x.shape
    ctx_len = ctx.shape[1]
    assert nb + 1 <= MOD_ROWS

    cvec = jnp.concatenate([c, c_ctx[None, :], jnp.zeros((MOD_ROWS - nb - 1, D_MODEL), F32)], axis=0)
    lamv = jnp.stack([lam_q1, lam_k1, lam_q2, lam_k2], axis=1).astype(F32)
    mods, lams = _modulation(cvec, w_mod, b_mod, lamv)
    tables = _rope_tables(seq)
    latent_row = lambda b: b
    ctx_row = lambda b: nb

    w_in_bf = _head_layout(w_in.astype(BF16))
    w_out_bf = w_out.astype(BF16)
    pool_w_bf = pool_w.astype(BF16)

    xl, xc = x, ctx
    for l in range(DEPTH):
        last = l == DEPTH - 1
        lam_init = 0.8 - 0.6 * math.exp(-0.3 * l)
        mods3 = mods[l].reshape(MOD_ROWS, 1, 3 * D_MODEL)
        gain = subln_g[l].reshape(1, V_DIM)
        p_l = _project(xl, mods3, latent_row, w_in_bf, l, 0, D_IN // PROJ_TN, PROJ_TM, tables)
        xc_flat = xc.reshape(1, nb * ctx_len, D_MODEL)
        if last:
            p_c = _project(xc_flat, mods3, ctx_row, w_in_bf, l, K_OFF // PROJ_TN, 2 * D_ATTN // PROJ_TN,
                           nb * ctx_len, None).reshape(nb, ctx_len, 2 * D_ATTN)
            kc_off, vc_off = 0, D_ATTN
        else:
            p_c = _project(xc_flat, mods3, ctx_row, w_in_bf, l, 0, D_IN // PROJ_TN, nb * ctx_len,
                           None).reshape(nb, ctx_len, D_IN)
            kc_off, vc_off = K_OFF, V_OFF
        y_attn = _attention(lams[l], gain, p_l, p_c, kc_off, vc_off, ATTN_TQ, lam_init)
        mix_args = (l, conv_w[l], pool_w_bf, pool_scale[l].reshape(1, D_POOL), w_out_bf,
                    ln_g[l].reshape(1, D_MODEL), ln_b[l].reshape(1, D_MODEL))
        xl_new = _mix(p_l, y_attn, xl, mods3, latent_row, *mix_args, MIX_TM)
        if not last:
            y_attn_c = _attention_ctx(lams[l], gain, p_c, lam_init)
            xc = _mix(p_c, y_attn_c, xc, mods3, ctx_row, *mix_args, ctx_len)
        xl = xl_new
    return xl
```

```python
import functools
import math

import numpy as np
import jax
import jax.numpy as jnp
from jax import lax
from jax.experimental import pallas as pl
from jax.experimental.pallas import tpu as pltpu

F32 = jnp.float32
BF16 = jnp.bfloat16

D_MODEL = 2048
DEPTH = 2
GRID_W = 64
D_CONV = 512
D_ATTN = 1024
D_POOL = 512
N_HEADS = 8
V_DIM = 128
QK_DIM = 64
ATTN_SCALE = QK_DIM ** -0.5
LOG2_E = math.log2(math.e)
Q_PRESCALE = ATTN_SCALE * LOG2_E
POOL_WINDOWS = (2, 4, 8, 16)
POOL_GROUP = 128
ROPE_THETA = 10000.0
LN_EPS = 1e-5
RMS_EPS = 1e-5
DEEPNORM_ALPHA = (2 * DEPTH) ** 0.25
D_IN = 4 * D_CONV + 4 * D_ATTN + 2 * D_POOL
Q_OFF = 4 * D_CONV
K_OFF = Q_OFF + D_ATTN
V_OFF = K_OFF + D_ATTN
G_OFF = V_OFF + D_ATTN
POOL_OFF = G_OFF + D_ATTN

LANES = 128
SUBLANES = 8
BF16_ROWS = 16
HALO = BF16_ROWS
MOD_ROWS = 8
VMEM_LIMIT = 56 * 1024 * 1024

PROJ_TM = 1024
PROJ_TN = 1024
ATTN_TQ = 256
ATTN_CHUNK = 256
MIX_TM = 512
MIX_SUB = 256
MOD_TN = 1536


def _silu(x):
    return x / (1.0 + jnp.exp(-x))


def _params(*sem):
    return pltpu.CompilerParams(dimension_semantics=sem, vmem_limit_bytes=VMEM_LIMIT)


def _mod_kernel(c_ref, w_ref, b_ref, lamv_ref, mod_ref, lam_ref):
    layer = pl.program_id(0)
    s = _silu(c_ref[...]).astype(BF16)
    w = w_ref[0].astype(BF16)
    mod_ref[0] = jnp.dot(s, w, preferred_element_type=F32) + b_ref[0]
    lv = lamv_ref[0]
    d1 = jnp.sum(lv[0:1] * lv[1:2], axis=-1, keepdims=True)
    d2 = jnp.sum(lv[2:3] * lv[3:4], axis=-1, keepdims=True)
    lf = jnp.zeros((1, LANES), F32) + layer.astype(F32)
    lam_init = 0.8 - 0.6 * jnp.exp(-0.3 * lf)
    lam_ref[0] = jnp.exp(d1) - jnp.exp(d2) + lam_init


def _modulation(cvec, w_mod, b_mod, lamv):
    n_out = w_mod.shape[-1]
    return pl.pallas_call(
        _mod_kernel,
        grid=(DEPTH, n_out // MOD_TN),
        in_specs=[
            pl.BlockSpec((MOD_ROWS, D_MODEL), lambda l, j: (0, 0)),
            pl.BlockSpec((1, D_MODEL, MOD_TN), lambda l, j: (l, 0, j)),
            pl.BlockSpec((1, 1, MOD_TN), lambda l, j: (l, 0, j)),
            pl.BlockSpec((1, 4, QK_DIM), lambda l, j: (l, 0, 0)),
        ],
        out_specs=[
            pl.BlockSpec((1, MOD_ROWS, MOD_TN), lambda l, j: (l, 0, j)),
            pl.BlockSpec((1, 1, LANES), lambda l, j: (l, 0, 0)),
        ],
        out_shape=[
            jax.ShapeDtypeStruct((DEPTH, MOD_ROWS, n_out), F32),
            jax.ShapeDtypeStruct((DEPTH, 1, LANES), F32),
        ],
        compiler_params=_params("arbitrary", "arbitrary"),
        name="modulation",
    )(cvec, w_mod, b_mod.reshape(DEPTH, 1, n_out), lamv)


def _proj_kernel(*refs, rope, q_blk, k_blk):
    if rope:
        x_ref, sh_ref, sc_ref, w_ref, cos_ref, sin_ref, o_ref, h_ref = refs
    else:
        x_ref, sh_ref, sc_ref, w_ref, o_ref, h_ref = refs
    j = pl.program_id(2)

    @pl.when(j == 0)
    def _():
        h_ref[...] = (x_ref[0] * (1.0 + sc_ref[0]) + sh_ref[0]).astype(BF16)

    acc = jnp.dot(h_ref[...], w_ref[...], preferred_element_type=F32)
    o_ref[0] = acc.astype(BF16)

    if rope:
        @pl.when((j == q_blk) | (j == k_blk))
        def _():
            scale = jnp.where(j == q_blk, Q_PRESCALE, 1.0).astype(F32)
            cos = cos_ref[...] * scale
            sin = sin_ref[...] * scale
            for c in range(acc.shape[1] // LANES):
                xc = acc[:, c * LANES:(c + 1) * LANES]
                r = xc * cos + pltpu.roll(xc, LANES // 2, 1) * sin
                o_ref[0, :, c * LANES:(c + 1) * LANES] = r.astype(BF16)


def _project(x, mods3, mod_row, w_bf16, layer, col_blk0, n_col_blks, tm, rope_tables):
    nb, t, _ = x.shape
    tn = PROJ_TN
    rope = rope_tables is not None
    in_specs = [
        pl.BlockSpec((1, tm, D_MODEL), lambda b, i, j: (b, i, 0)),
        pl.BlockSpec((1, 1, D_MODEL), lambda b, i, j: (mod_row(b), 0, 0)),
        pl.BlockSpec((1, 1, D_MODEL), lambda b, i, j: (mod_row(b), 0, 1)),
        pl.BlockSpec((None, D_MODEL, tn), lambda b, i, j: (layer, 0, j + col_blk0)),
    ]
    args = [x, mods3, mods3, w_bf16]
    if rope:
        in_specs += [pl.BlockSpec((tm, LANES), lambda b, i, j: (i, 0))] * 2
        args += list(rope_tables)
    return pl.pallas_call(
        functools.partial(_proj_kernel, rope=rope, q_blk=Q_OFF // tn - col_blk0, k_blk=K_OFF // tn - col_blk0),
        grid=(nb, t // tm, n_col_blks),
        in_specs=in_specs,
        out_specs=pl.BlockSpec((1, tm, tn), lambda b, i, j: (b, i, j)),
        out_shape=jax.ShapeDtypeStruct((nb, t, n_col_blks * tn), BF16),
        scratch_shapes=[pltpu.VMEM((tm, D_MODEL), BF16)],
        compiler_params=_params("arbitrary", "arbitrary", "arbitrary"),
        name="proj_rope" if rope else "proj",
    )(*args)


def _head_layout(w):
    n_layers, d, _ = w.shape
    n_freq = QK_DIM // 4
    qk = w[:, :, Q_OFF:V_OFF].reshape(n_layers, d, 2, N_HEADS, 2, 2, 2, n_freq)
    qk = qk.transpose(0, 1, 2, 3, 6, 4, 5, 7).reshape(n_layers, d, 2 * D_ATTN)
    return lax.dynamic_update_slice(w, qk, (0, 0, Q_OFF))


def _split_halves(q):
    lane = lax.broadcasted_iota(jnp.int32, (1, LANES), 1)
    first = (lane // (QK_DIM // 2)) % 2 == 0
    zero = jnp.zeros_like(q)
    return jnp.concatenate([jnp.where(first, q, zero), jnp.where(first, zero, q)], axis=0)


def _attn_ctx_kernel(lam_ref, gain_ref, q_ref, ag_ref, k_ref, v_ref, o_ref, *, lam_init):
    tq = q_ref.shape[1]
    for h in range(N_HEADS):
        cols = slice(h * V_DIM, (h + 1) * V_DIM)
        qs = _split_halves(q_ref[0, :, cols] * ATTN_SCALE)
        s = lax.dot_general(qs, k_ref[0, :, cols], (((1,), (1,)), ((), ())), preferred_element_type=F32)
        p = jnp.exp(s - s.max(axis=-1, keepdims=True))
        den = p.sum(axis=-1, keepdims=True)
        den1, den2 = den[:tq], den[tq:]
        ratio = lam_ref[:, 0:1] * den1 / den2
        a = (p[:tq] - ratio * p[tq:]).astype(BF16)
        o = jnp.dot(a, v_ref[0, :, cols], preferred_element_type=F32) / den1
        y = o * lax.rsqrt(jnp.mean(o * o, axis=-1, keepdims=True) + RMS_EPS) * (gain_ref[...] * (1.0 - lam_init))
        o_ref[0, :, cols] = (_silu(ag_ref[0, :, cols].astype(F32)) * y).astype(BF16)


def _attention_ctx(lam, gain, p_c, lam_init):
    nb, t, _ = p_c.shape
    col = lambda off: (lambda b: (b, 0, off // D_ATTN))
    return pl.pallas_call(
        functools.partial(_attn_ctx_kernel, lam_init=lam_init),
        grid=(nb,),
        in_specs=[
            pl.BlockSpec((1, LANES), lambda b: (0, 0)),
            pl.BlockSpec((1, V_DIM), lambda b: (0, 0)),
            pl.BlockSpec((1, t, D_ATTN), col(Q_OFF)),
            pl.BlockSpec((1, t, D_ATTN), col(G_OFF)),
            pl.BlockSpec((1, t, D_ATTN), col(K_OFF)),
            pl.BlockSpec((1, t, D_ATTN), col(V_OFF)),
        ],
        out_specs=pl.BlockSpec((1, t, D_ATTN), col(0)),
        out_shape=jax.ShapeDtypeStruct((nb, t, D_ATTN), BF16),
        compiler_params=_params("arbitrary"),
        name="diff_attn_ctx",
    )(lam, gain, p_c, p_c, p_c, p_c)


def _attn_kernel(lam_ref, gain_ref, q_ref, ag_ref, kc_ref, kl_ref, vc_ref, vl_ref, o_ref,
                 s0, s1, m0, m1, p0, p1, l0, l1, *, lam_init):
    step = pl.program_id(0)
    tq = q_ref.shape[1] // 2
    nt = (((1,), (1,)), ((), ()))
    tn = (((0,), (0,)), ((), ()))
    chunks = ([(kc_ref, vc_ref, r) for r in range(0, kc_ref.shape[1], ATTN_CHUNK)]
              + [(kl_ref, vl_ref, r) for r in range(0, kl_ref.shape[1], ATTN_CHUNK)])
    groups = ATTN_CHUNK // SUBLANES

    @pl.when(step == 0)
    def _():
        s1[...] = jnp.zeros_like(s1)
        m1[...] = jnp.zeros_like(m1)
        p0[...] = jnp.zeros_like(p0)
        l0[...] = jnp.ones_like(l0)

    def half(tile_rows, s_w, m_w, s_r, m_r, p_w, l_w, p_r, l_r):
        qs = _split_halves(q_ref[0, tile_rows, :])
        m_prev = m_r[...]
        den = l_r[...]
        den1, den2 = den[:, :tq], den[:, tq:]
        ratio = (lam_ref[:, 0:1] * den1 / den2).astype(BF16)
        m_acc = jnp.full((SUBLANES, 2 * tq), -jnp.inf, F32)
        l_acc = jnp.zeros((SUBLANES, 2 * tq), F32)
        o_t = jnp.zeros((V_DIM, tq), F32)
        scores = {id(r): lax.dot_general(r[0], qs, nt, preferred_element_type=F32) for r in (kc_ref, kl_ref)}
        for c, (k_ref, v_ref, r0) in enumerate(chunks):
            rows = slice(c * ATTN_CHUNK, (c + 1) * ATTN_CHUNK)
            s = scores[id(k_ref)][r0:r0 + ATTN_CHUNK]
            s_w[rows, :] = s
            pace = jnp.where(l_acc < 0.0, 1.0, 0.0)
            m_acc = jnp.maximum(m_acc + pace, s.reshape(groups, SUBLANES, 2 * tq).max(axis=0))
            p = jnp.exp2(s_r[rows, :] - m_prev)
            l_acc = l_acc + p.reshape(groups, SUBLANES, 2 * tq).sum(axis=0)
            p_w[rows, :] = p.astype(BF16)
            pr = p_r[rows, :]
            a = pr[:, :tq] - ratio * pr[:, tq:]
            o_t = o_t + lax.dot_general(v_ref[0, r0:r0 + ATTN_CHUNK, :], a, tn, preferred_element_type=F32)
        m_w[...] = m_acc.max(axis=0, keepdims=True)
        l_w[...] = l_acc.sum(axis=0, keepdims=True)
        o = (o_t * (1.0 / den1)).T
        y = o * lax.rsqrt(jnp.mean(o * o, axis=-1, keepdims=True) + RMS_EPS) * (gain_ref[...] * (1.0 - lam_init))
        o_ref[0, tile_rows, :] = (_silu(ag_ref[0, tile_rows, :].astype(F32)) * y).astype(BF16)

    half(slice(0, tq), s0, m0, s1, m1, p1, l1, p0, l0)
    half(slice(tq, 2 * tq), s1, m1, s0, m0, p0, l0, p1, l1)


def _attention(lam, gain, p_l, p_c, kc_off, vc_off, tq, lam_init):
    nb, seq, _ = p_l.shape
    n_ctx = p_c.shape[1]
    assert n_ctx % ATTN_CHUNK == 0 and seq % ATTN_CHUNK == 0 and seq % (2 * tq) == 0
    n_keys = n_ctx + seq
    pairs_per_head = seq // (2 * tq)
    n_pairs = nb * N_HEADS * pairs_per_head

    def pair(k):
        k = jnp.clip(k, 0, n_pairs - 1)
        return k // (N_HEADS * pairs_per_head), (k // pairs_per_head) % N_HEADS, k % pairs_per_head

    def tile_map(off, lag):
        def f(k):
            b, h, i = pair(k - lag)
            return b, i, off // V_DIM + h
        return f

    def head_map(off, lag):
        def f(k):
            b, h, _ = pair(k - lag)
            return b, 0, off // V_DIM + h
        return f

    return pl.pallas_call(
        functools.partial(_attn_kernel, lam_init=lam_init),
        grid=(n_pairs + 1,),
        in_specs=[
            pl.BlockSpec((1, LANES), lambda k: (0, 0)),
            pl.BlockSpec((1, V_DIM), lambda k: (0, 0)),
            pl.BlockSpec((1, 2 * tq, V_DIM), tile_map(Q_OFF, 0)),
            pl.BlockSpec((1, 2 * tq, V_DIM), tile_map(G_OFF, 1)),
            pl.BlockSpec((1, n_ctx, V_DIM), head_map(kc_off, 0)),
            pl.BlockSpec((1, seq, V_DIM), head_map(K_OFF, 0)),
            pl.BlockSpec((1, n_ctx, V_DIM), head_map(vc_off, 1)),
            pl.BlockSpec((1, seq, V_DIM), head_map(V_OFF, 1)),
        ],
        out_specs=pl.BlockSpec((1, 2 * tq, V_DIM), tile_map(0, 1)),
        out_shape=jax.ShapeDtypeStruct((nb, seq, D_ATTN), BF16),
        scratch_shapes=[
            pltpu.VMEM((n_keys, 2 * tq), F32), pltpu.VMEM((n_keys, 2 * tq), F32),
            pltpu.VMEM((1, 2 * tq), F32), pltpu.VMEM((1, 2 * tq), F32),
            pltpu.VMEM((n_keys, 2 * tq), BF16), pltpu.VMEM((n_keys, 2 * tq), BF16),
            pltpu.VMEM((1, 2 * tq), F32), pltpu.VMEM((1, 2 * tq), F32),
        ],
        compiler_params=_params("arbitrary"),
        name="diff_attn",
    )(lam, gain, p_l, p_l, p_c, p_l, p_c, p_l)


def _mix_kernel(pc_ref, cup_ref, ccp_ref, cun_ref, ccn_ref, pp_ref, pup_ref, pun_ref, ya_ref, x_ref, g_ref,
                cw_ref, pw_ref, ps_ref, wo_ref, lg_ref, lb_ref, o_ref, mix_ref, *, seq_len):
    i = pl.program_id(1)
    tm = x_ref.shape[1]
    n_tiles = seq_len // tm
    has_prev = i > 0
    has_next = i < n_tiles - 1
    sub = min(MIX_SUB, tm)
    n_ext = sub + 2 * HALO
    mid = slice(HALO, HALO + sub)
    cw = cw_ref[...]

    def ext(ref, col0, width, prev_ref, next_ref, r0):
        cols = slice(col0, col0 + width)
        if r0 == 0:
            before = jnp.where(has_prev, prev_ref[0].astype(F32), 0.0)
        else:
            before = ref[0, r0 - HALO:r0, cols].astype(F32)
        if r0 + sub == tm:
            after = jnp.where(has_next, next_ref[0].astype(F32), 0.0)
        else:
            after = ref[0, r0 + sub:r0 + sub + HALO, cols].astype(F32)
        return jnp.concatenate([before, ref[0, r0:r0 + sub, cols].astype(F32), after], axis=0)

    pace = None
    for r0 in range(0, tm, sub):
        rs = slice(r0, r0 + sub)
        v = ext(pc_ref, 2 * D_CONV, D_CONV, ccp_ref, ccn_ref, r0) * ext(pc_ref, 0, D_CONV, cup_ref, cun_ref, r0)
        u_ext = ext(pp_ref, 0, D_POOL, pup_ref, pun_ref, r0)
        if pace is not None:
            v = v + pace
            u_ext = u_ext + pace
        y = pltpu.roll(v, 1, 0)[mid] * cw[0:1] + v[mid] * cw[1:2] + pltpu.roll(v, n_ext - 1, 0)[mid] * cw[2:3]
        cb = pc_ref[0, rs, D_CONV:2 * D_CONV].astype(F32)
        cg = pc_ref[0, rs, 3 * D_CONV:4 * D_CONV].astype(F32)
        mix_ref[rs, 0:D_CONV] = (_silu(cg) * (cb * y)).astype(BF16)

        mix_ref[rs, D_CONV:D_CONV + D_ATTN] = ya_ref[0, rs, :]

        u = u_ext[mid]
        pg = pp_ref[0, rs, D_POOL:2 * D_POOL].astype(F32)
        t = i * tm + r0 + lax.broadcasted_iota(jnp.int32, (sub, 1), 0)
        for gi, w in enumerate(POOL_WINDOWS):
            sl = slice(gi * POOL_GROUP, (gi + 1) * POOL_GROUP)
            s = u_ext[:, sl]
            k = 1
            while k < w:
                s = s + pltpu.roll(s, n_ext - k, 0)
                k *= 2
            win = pltpu.roll(s, w // 2, 0)[mid]
            lo = jnp.maximum(t - w // 2, 0)
            hi = jnp.minimum(t + w - w // 2, seq_len)
            d = (win / (hi - lo).astype(F32) - u[:, sl]).astype(BF16)
            yg = jnp.dot(d, pw_ref[gi], preferred_element_type=F32)
            y_pool = _silu(pg[:, sl]) * (yg * ps_ref[:, sl])
            mix_ref[rs, D_CONV + D_ATTN + gi * POOL_GROUP:D_CONV + D_ATTN + (gi + 1) * POOL_GROUP] = (
                y_pool.astype(BF16))

        out = jnp.dot(mix_ref[rs, :], wo_ref[...], preferred_element_type=F32)
        pace = jnp.where(out[0:1, :D_CONV] < -jnp.inf, 1.0, 0.0)
        z = DEEPNORM_ALPHA * x_ref[0, rs, :] + g_ref[0] * out
        mu = jnp.mean(z, axis=-1, keepdims=True)
        zc = z - mu
        var = jnp.mean(zc * zc, axis=-1, keepdims=True)
        o_ref[0, rs, :] = zc * lax.rsqrt(var + LN_EPS) * lg_ref[...] + lb_ref[...]


def _mix(p, y_attn, x, mods3, mod_row, layer, conv_w, pool_w_bf16, pool_scale, w_out_bf16, ln_g, ln_b, tm):
    nb, t, _ = x.shape
    hb = tm // HALO
    last = t // HALO - 1
    prev = lambda i: jnp.maximum(i * hb - 1, 0)
    nxt = lambda i: jnp.minimum((i + 1) * hb, last)
    cc_blk = 2
    const2 = lambda b, i: (0, 0)
    in_specs = [
        pl.BlockSpec((1, tm, 4 * D_CONV), lambda b, i: (b, i, 0)),
        pl.BlockSpec((1, HALO, D_CONV), lambda b, i: (b, prev(i), 0)),
        pl.BlockSpec((1, HALO, D_CONV), lambda b, i: (b, prev(i), cc_blk)),
        pl.BlockSpec((1, HALO, D_CONV), lambda b, i: (b, nxt(i), 0)),
        pl.BlockSpec((1, HALO, D_CONV), lambda b, i: (b, nxt(i), cc_blk)),
        pl.BlockSpec((1, tm, 2 * D_POOL), lambda b, i: (b, i, POOL_OFF // (2 * D_POOL))),
        pl.BlockSpec((1, HALO, D_POOL), lambda b, i: (b, prev(i), POOL_OFF // D_POOL)),
        pl.BlockSpec((1, HALO, D_POOL), lambda b, i: (b, nxt(i), POOL_OFF // D_POOL)),
        pl.BlockSpec((1, tm, D_ATTN), lambda b, i: (b, i, 0)),
        pl.BlockSpec((1, tm, D_MODEL), lambda b, i: (b, i, 0)),
        pl.BlockSpec((1, 1, D_MODEL), lambda b, i: (mod_row(b), 0, 2)),
        pl.BlockSpec(conv_w.shape, const2),
        pl.BlockSpec((None,) + pool_w_bf16.shape[1:], lambda b, i: (layer, 0, 0, 0)),
        pl.BlockSpec(pool_scale.shape, const2),
        pl.BlockSpec((None,) + w_out_bf16.shape[1:], lambda b, i: (layer, 0, 0), pipeline_mode=pl.Buffered(1)),
        pl.BlockSpec(ln_g.shape, const2),
        pl.BlockSpec(ln_b.shape, const2),
    ]
    return pl.pallas_call(
        functools.partial(_mix_kernel, seq_len=t),
        grid=(nb, t // tm),
        in_specs=in_specs,
        out_specs=pl.BlockSpec((1, tm, D_MODEL), lambda b, i: (b, i, 0)),
        out_shape=jax.ShapeDtypeStruct((nb, t, D_MODEL), F32),
        scratch_shapes=[pltpu.VMEM((tm, D_MODEL), BF16)],
        compiler_params=_params("arbitrary", "arbitrary"),
        name="mix_out_ln",
    )(p, p, p, p, p, p, p, p, y_attn, x, mods3, conv_w, pool_w_bf16, pool_scale, w_out_bf16, ln_g, ln_b)


def _rope_tables(seq_len):
    n = QK_DIM // 4
    inv = ROPE_THETA ** (-np.arange(n, dtype=np.float64) / n)
    tok = np.arange(seq_len)
    pos = np.stack([tok // GRID_W, tok % GRID_W], axis=1).astype(np.float64)
    m = np.arange(LANES)
    ang = pos[:, (m // n) % 2] * inv[m % n][None, :]
    second = (m >= LANES // 2)[None, :]
    cos, sin = np.cos(ang), np.sin(ang)
    return jnp.asarray(cos, dtype=F32), jnp.asarray(np.where(second, sin, -sin), dtype=F32)


def kernel(x, c, ctx, c_ctx, w_mod, b_mod, w_in, conv_w, lam_q1, lam_k1, lam_q2, lam_k2, subln_g, pool_w,
           pool_scale, w_out, ln_g, ln_b):
    nb, seq, _ = x.shape
    ctx_len = ctx.shape[1]
    assert nb + 1 <= MOD_ROWS

    cvec = jnp.concatenate([c, c_ctx[None, :], jnp.zeros((MOD_ROWS - nb - 1, D_MODEL), F32)], axis=0)
    lamv = jnp.stack([lam_q1, lam_k1, lam_q2, lam_k2], axis=1).astype(F32)
    mods, lams = _modulation(cvec, w_mod, b_mod, lamv)
    tables = _rope_tables(seq)
    latent_row = lambda b: b
    ctx_row = lambda b: nb

    w_in_bf = _head_layout(w_in.astype(BF16))
    w_out_bf = w_out.astype(BF16)
    pool_w_bf = pool_w.astype(BF16)

    xl, xc = x, ctx
    for l in range(DEPTH):
        last = l == DEPTH - 1
        lam_init = 0.8 - 0.6 * math.exp(-0.3 * l)
        mods3 = mods[l].reshape(MOD_ROWS, 1, 3 * D_MODEL)
        gain = subln_g[l].reshape(1, V_DIM)
        p_l = _project(xl, mods3, latent_row, w_in_bf, l, 0, D_IN // PROJ_TN, PROJ_TM, tables)
        xc_flat = xc.reshape(1, nb * ctx_len, D_MODEL)
        if last:
            p_c = _project(xc_flat, mods3, ctx_row, w_in_bf, l, K_OFF // PROJ_TN, 2 * D_ATTN // PROJ_TN,
                           nb * ctx_len, None).reshape(nb, ctx_len, 2 * D_ATTN)
            kc_off, vc_off = 0, D_ATTN
        else:
            p_c = _project(xc_flat, mods3, ctx_row, w_in_bf, l, 0, D_IN // PROJ_TN, nb * ctx_len,
                           None).reshape(nb, ctx_len, D_IN)
            kc_off, vc_off = K_OFF, V_OFF
        y_attn = _attention(lams[l], gain, p_l, p_c, kc_off, vc_off, ATTN_TQ, lam_init)
        mix_args = (l, conv_w[l], pool_w_bf, pool_scale[l].reshape(1, D_POOL), w_out_bf,
                    ln_g[l].reshape(1, D_MODEL), ln_b[l].reshape(1, D_MODEL))
        xl_new = _mix(p_l, y_attn, xl, mods3, latent_row, *mix_args, MIX_TM)
        if not last:
            y_attn_c = _attention_ctx(lams[l], gain, p_c, lam_init)
            xc = _mix(p_c, y_attn_c, xc, mods3, ctx_row, *mix_args, ctx_len)
        xl = xl_new
    return xl
```

```python
import functools
import math

import numpy as np
import jax
import jax.numpy as jnp
from jax import lax
from jax.experimental import pallas as pl
from jax.experimental.pallas import tpu as pltpu

F32 = jnp.float32
BF16 = jnp.bfloat16

D_MODEL = 2048
DEPTH = 2
GRID_W = 64
D_CONV = 512
D_ATTN = 1024
D_POOL = 512
N_HEADS = 8
V_DIM = 128
QK_DIM = 64
ATTN_SCALE = QK_DIM ** -0.5
LOG2_E = math.log2(math.e)
Q_PRESCALE = ATTN_SCALE * LOG2_E
POOL_WINDOWS = (2, 4, 8, 16)
POOL_GROUP = 128
ROPE_THETA = 10000.0
LN_EPS = 1e-5
RMS_EPS = 1e-5
DEEPNORM_ALPHA = (2 * DEPTH) ** 0.25
D_IN = 4 * D_CONV + 4 * D_ATTN + 2 * D_POOL
Q_OFF = 4 * D_CONV
K_OFF = Q_OFF + D_ATTN
V_OFF = K_OFF + D_ATTN
G_OFF = V_OFF + D_ATTN
POOL_OFF = G_OFF + D_ATTN

LANES = 128
SUBLANES = 8
BF16_ROWS = 16
HALO = BF16_ROWS
MOD_ROWS = 8
VMEM_LIMIT = 56 * 1024 * 1024

PROJ_TM = 1024
PROJ_TN = 1024
ATTN_TQ = 256
ATTN_CHUNK = 256
MIX_TM = 512
MIX_SUB = 256
MOD_TN = 1536


def _silu(x):
    return x / (1.0 + jnp.exp(-x))


def _params(*sem):
    return pltpu.CompilerParams(dimension_semantics=sem, vmem_limit_bytes=VMEM_LIMIT)


def _mod_kernel(c_ref, w_ref, b_ref, lamv_ref, mod_ref, lam_ref):
    layer = pl.program_id(0)
    s = _silu(c_ref[...]).astype(BF16)
    w = w_ref[0].astype(BF16)
    mod_ref[0] = jnp.dot(s, w, preferred_element_type=F32) + b_ref[0]
    lv = lamv_ref[0]
    d1 = jnp.sum(lv[0:1] * lv[1:2], axis=-1, keepdims=True)
    d2 = jnp.sum(lv[2:3] * lv[3:4], axis=-1, keepdims=True)
    lf = jnp.zeros((1, LANES), F32) + layer.astype(F32)
    lam_init = 0.8 - 0.6 * jnp.exp(-0.3 * lf)
    lam_ref[0] = jnp.exp(d1) - jnp.exp(d2) + lam_init


def _modulation(cvec, w_mod, b_mod, lamv):
    n_out = w_mod.shape[-1]
    return pl.pallas_call(
        _mod_kernel,
        grid=(DEPTH, n_out // MOD_TN),
        in_specs=[
            pl.BlockSpec((MOD_ROWS, D_MODEL), lambda l, j: (0, 0)),
            pl.BlockSpec((1, D_MODEL, MOD_TN), lambda l, j: (l, 0, j)),
            pl.BlockSpec((1, 1, MOD_TN), lambda l, j: (l, 0, j)),
            pl.BlockSpec((1, 4, QK_DIM), lambda l, j: (l, 0, 0)),
        ],
        out_specs=[
            pl.BlockSpec((1, MOD_ROWS, MOD_TN), lambda l, j: (l, 0, j)),
            pl.BlockSpec((1, 1, LANES), lambda l, j: (l, 0, 0)),
        ],
        out_shape=[
            jax.ShapeDtypeStruct((DEPTH, MOD_ROWS, n_out), F32),
            jax.ShapeDtypeStruct((DEPTH, 1, LANES), F32),
        ],
        compiler_params=_params("arbitrary", "arbitrary"),
        name="modulation",
    )(cvec, w_mod, b_mod.reshape(DEPTH, 1, n_out), lamv)


def _proj_kernel(*refs, rope, q_blk, k_blk):
    if rope:
        x_ref, sh_ref, sc_ref, w_ref, cos_ref, sin_ref, o_ref, h_ref = refs
    else:
        x_ref, sh_ref, sc_ref, w_ref, o_ref, h_ref = refs
    j = pl.program_id(2)

    @pl.when(j == 0)
    def _():
        h_ref[...] = (x_ref[0] * (1.0 + sc_ref[0]) + sh_ref[0]).astype(BF16)

    tm, tn = o_ref.shape[1], o_ref.shape[2]
    half = tm // 2
    blocks = (slice(0, half), slice(half, tm))

    def matmul_blocks():
        accs, pace = [], None
        for rs in blocks:
            lhs = h_ref[rs, :]
            if pace is not None:
                lhs = lhs + pace
            acc = jnp.dot(lhs, w_ref[...], preferred_element_type=F32)
            zero = jnp.where(acc[0:1, :] < -jnp.inf, 1.0, 0.0).astype(BF16)
            pace = jnp.concatenate([zero] * (h_ref.shape[1] // tn), axis=1)
            accs.append(acc)
        return accs

    def plain():
        for rs, acc in zip(blocks, matmul_blocks()):
            o_ref[0, rs, :] = acc.astype(BF16)

    if not rope:
        plain()
    else:
        is_rope = (j == q_blk) | (j == k_blk)
        pl.when(jnp.logical_not(is_rope))(plain)

        @pl.when(is_rope)
        def _():
            scale = jnp.where(j == q_blk, Q_PRESCALE, 1.0).astype(F32)
            for rs, acc in zip(blocks, matmul_blocks()):
                cos = cos_ref[rs, :] * scale
                sin = sin_ref[rs, :] * scale
                for c in range(tn // LANES):
                    xc = acc[:, c * LANES:(c + 1) * LANES]
                    r = xc * cos + pltpu.roll(xc, LANES // 2, 1) * sin
                    o_ref[0, rs, c * LANES:(c + 1) * LANES] = r.astype(BF16)


def _project(x, mods3, mod_row, w_bf16, layer, col_blk0, n_col_blks, tm, rope_tables):
    nb, t, _ = x.shape
    tn = PROJ_TN
    rope = rope_tables is not None
    in_specs = [
        pl.BlockSpec((1, tm, D_MODEL), lambda b, i, j: (b, i, 0)),
        pl.BlockSpec((1, 1, D_MODEL), lambda b, i, j: (mod_row(b), 0, 0)),
        pl.BlockSpec((1, 1, D_MODEL), lambda b, i, j: (mod_row(b), 0, 1)),
        pl.BlockSpec((None, D_MODEL, tn), lambda b, i, j: (layer, 0, j + col_blk0)),
    ]
    args = [x, mods3, mods3, w_bf16]
    if rope:
        in_specs += [pl.BlockSpec((tm, LANES), lambda b, i, j: (i, 0))] * 2
        args += list(rope_tables)
    return pl.pallas_call(
        functools.partial(_proj_kernel, rope=rope, q_blk=Q_OFF // tn - col_blk0, k_blk=K_OFF // tn - col_blk0),
        grid=(nb, t // tm, n_col_blks),
        in_specs=in_specs,
        out_specs=pl.BlockSpec((1, tm, tn), lambda b, i, j: (b, i, j)),
        out_shape=jax.ShapeDtypeStruct((nb, t, n_col_blks * tn), BF16),
        scratch_shapes=[pltpu.VMEM((tm, D_MODEL), BF16)],
        compiler_params=_params("arbitrary", "arbitrary", "arbitrary"),
        name="proj_rope" if rope else "proj",
    )(*args)


def _head_layout(w):
    n_layers, d, _ = w.shape
    n_freq = QK_DIM // 4
    qk = w[:, :, Q_OFF:V_OFF].reshape(n_layers, d, 2, N_HEADS, 2, 2, 2, n_freq)
    qk = qk.transpose(0, 1, 2, 3, 6, 4, 5, 7).reshape(n_layers, d, 2 * D_ATTN)
    return lax.dynamic_update_slice(w, qk, (0, 0, Q_OFF))


def _split_halves(q):
    lane = lax.broadcasted_iota(jnp.int32, (1, LANES), 1)
    first = (lane // (QK_DIM // 2)) % 2 == 0
    zero = jnp.zeros_like(q)
    return jnp.concatenate([jnp.where(first, q, zero), jnp.where(first, zero, q)], axis=0)


def _attn_ctx_kernel(lam_ref, gain_ref, q_ref, ag_ref, k_ref, v_ref, o_ref, *, lam_init):
    tq = q_ref.shape[1]
    for h in range(N_HEADS):
        cols = slice(h * V_DIM, (h + 1) * V_DIM)
        qs = _split_halves(q_ref[0, :, cols] * ATTN_SCALE)
        s = lax.dot_general(qs, k_ref[0, :, cols], (((1,), (1,)), ((), ())), preferred_element_type=F32)
        p = jnp.exp(s - s.max(axis=-1, keepdims=True))
        den = p.sum(axis=-1, keepdims=True)
        den1, den2 = den[:tq], den[tq:]
        ratio = lam_ref[:, 0:1] * den1 / den2
        a = (p[:tq] - ratio * p[tq:]).astype(BF16)
        o = jnp.dot(a, v_ref[0, :, cols], preferred_element_type=F32) / den1
        y = o * lax.rsqrt(jnp.mean(o * o, axis=-1, keepdims=True) + RMS_EPS) * (gain_ref[...] * (1.0 - lam_init))
        o_ref[0, :, cols] = (_silu(ag_ref[0, :, cols].astype(F32)) * y).astype(BF16)


def _attention_ctx(lam, gain, p_c, lam_init):
    nb, t, _ = p_c.shape
    col = lambda off: (lambda b: (b, 0, off // D_ATTN))
    return pl.pallas_call(
        functools.partial(_attn_ctx_kernel, lam_init=lam_init),
        grid=(nb,),
        in_specs=[
            pl.BlockSpec((1, LANES), lambda b: (0, 0)),
            pl.BlockSpec((1, V_DIM), lambda b: (0, 0)),
            pl.BlockSpec((1, t, D_ATTN), col(Q_OFF)),
            pl.BlockSpec((1, t, D_ATTN), col(G_OFF)),
            pl.BlockSpec((1, t, D_ATTN), col(K_OFF)),
            pl.BlockSpec((1, t, D_ATTN), col(V_OFF)),
        ],
        out_specs=pl.BlockSpec((1, t, D_ATTN), col(0)),
        out_shape=jax.ShapeDtypeStruct((nb, t, D_ATTN), BF16),
        compiler_params=_params("arbitrary"),
        name="diff_attn_ctx",
    )(lam, gain, p_c, p_c, p_c, p_c)


def _attn_kernel(lam_ref, gain_ref, q_ref, ag_ref, kc_ref, kl_ref, vc_ref, vl_ref, o_ref,
                 s0, s1, m0, m1, p0, p1, l0, l1, *, lam_init):
    step = pl.program_id(0)
    tq = q_ref.shape[1] // 2
    nt = (((1,), (1,)), ((), ()))
    tn = (((0,), (0,)), ((), ()))
    chunks = ([(kc_ref, vc_ref, r) for r in range(0, kc_ref.shape[1], ATTN_CHUNK)]
              + [(kl_ref, vl_ref, r) for r in range(0, kl_ref.shape[1], ATTN_CHUNK)])
    groups = ATTN_CHUNK // SUBLANES

    @pl.when(step == 0)
    def _():
        s1[...] = jnp.zeros_like(s1)
        m1[...] = jnp.zeros_like(m1)
        p0[...] = jnp.zeros_like(p0)
        l0[...] = jnp.ones_like(l0)

    def half(tile_rows, s_w, m_w, s_r, m_r, p_w, l_w, p_r, l_r):
        qs = _split_halves(q_ref[0, tile_rows, :])
        m_prev = m_r[...]
        den = l_r[...]
        den1, den2 = den[:, :tq], den[:, tq:]
        ratio = (lam_ref[:, 0:1] * den1 / den2).astype(BF16)
        m_acc = jnp.full((SUBLANES, 2 * tq), -jnp.inf, F32)
        l_acc = jnp.zeros((SUBLANES, 2 * tq), F32)
        o_t = jnp.zeros((V_DIM, tq), F32)
        scores = {id(r): lax.dot_general(r[0], qs, nt, preferred_element_type=F32) for r in (kc_ref, kl_ref)}
        for c, (k_ref, v_ref, r0) in enumerate(chunks):
            rows = slice(c * ATTN_CHUNK, (c + 1) * ATTN_CHUNK)
            s = scores[id(k_ref)][r0:r0 + ATTN_CHUNK]
            s_w[rows, :] = s
            pace = jnp.where(l_acc < 0.0, 1.0, 0.0)
            m_acc = jnp.maximum(m_acc + pace, s.reshape(groups, SUBLANES, 2 * tq).max(axis=0))
            p = jnp.exp2(s_r[rows, :] - m_prev)
            l_acc = l_acc + p.reshape(groups, SUBLANES, 2 * tq).sum(axis=0)
            p_w[rows, :] = p.astype(BF16)
            pr = p_r[rows, :]
            a = pr[:, :tq] - ratio * pr[:, tq:]
            o_t = o_t + lax.dot_general(v_ref[0, r0:r0 + ATTN_CHUNK, :], a, tn, preferred_element_type=F32)
        m_w[...] = m_acc.max(axis=0, keepdims=True)
        l_w[...] = l_acc.sum(axis=0, keepdims=True)
        o = (o_t * (1.0 / den1)).T
        y = o * lax.rsqrt(jnp.mean(o * o, axis=-1, keepdims=True) + RMS_EPS) * (gain_ref[...] * (1.0 - lam_init))
        o_ref[0, tile_rows, :] = (_silu(ag_ref[0, tile_rows, :].astype(F32)) * y).astype(BF16)

    half(slice(0, tq), s0, m0, s1, m1, p1, l1, p0, l0)
    half(slice(tq, 2 * tq), s1, m1, s0, m0, p0, l0, p1, l1)


def _attention(lam, gain, p_l, p_c, kc_off, vc_off, tq, lam_init):
    nb, seq, _ = p_l.shape
    n_ctx = p_c.shape[1]
    assert n_ctx % ATTN_CHUNK == 0 and seq % ATTN_CHUNK == 0 and seq % (2 * tq) == 0
    n_keys = n_ctx + seq
    pairs_per_head = seq // (2 * tq)
    n_pairs = nb * N_HEADS * pairs_per_head

    def pair(k):
        k = jnp.clip(k, 0, n_pairs - 1)
        return k // (N_HEADS * pairs_per_head), (k // pairs_per_head) % N_HEADS, k % pairs_per_head

    def tile_map(off, lag):
        def f(k):
            b, h, i = pair(k - lag)
            return b, i, off // V_DIM + h
        return f

    def head_map(off, lag):
        def f(k):
            b, h, _ = pair(k - lag)
            return b, 0, off // V_DIM + h
        return f

    return pl.pallas_call(
        functools.partial(_attn_kernel, lam_init=lam_init),
        grid=(n_pairs + 1,),
        in_specs=[
            pl.BlockSpec((1, LANES), lambda k: (0, 0)),
            pl.BlockSpec((1, V_DIM), lambda k: (0, 0)),
            pl.BlockSpec((1, 2 * tq, V_DIM), tile_map(Q_OFF, 0)),
            pl.BlockSpec((1, 2 * tq, V_DIM), tile_map(G_OFF, 1)),
            pl.BlockSpec((1, n_ctx, V_DIM), head_map(kc_off, 0)),
            pl.BlockSpec((1, seq, V_DIM), head_map(K_OFF, 0)),
            pl.BlockSpec((1, n_ctx, V_DIM), head_map(vc_off, 1)),
            pl.BlockSpec((1, seq, V_DIM), head_map(V_OFF, 1)),
        ],
        out_specs=pl.BlockSpec((1, 2 * tq, V_DIM), tile_map(0, 1)),
        out_shape=jax.ShapeDtypeStruct((nb, seq, D_ATTN), BF16),
        scratch_shapes=[
            pltpu.VMEM((n_keys, 2 * tq), F32), pltpu.VMEM((n_keys, 2 * tq), F32),
            pltpu.VMEM((1, 2 * tq), F32), pltpu.VMEM((1, 2 * tq), F32),
            pltpu.VMEM((n_keys, 2 * tq), BF16), pltpu.VMEM((n_keys, 2 * tq), BF16),
            pltpu.VMEM((1, 2 * tq), F32), pltpu.VMEM((1, 2 * tq), F32),
        ],
        compiler_params=_params("arbitrary"),
        name="diff_attn",
    )(lam, gain, p_l, p_l, p_c, p_l, p_c, p_l)


def _mix_kernel(pc_ref, cup_ref, ccp_ref, cun_ref, ccn_ref, pp_ref, pup_ref, pun_ref, ya_ref, x_ref, g_ref,
                cw_ref, pw_ref, ps_ref, wo_ref, lg_ref, lb_ref, o_ref, mix_ref, *, seq_len):
    i = pl.program_id(1)
    tm = x_ref.shape[1]
    n_tiles = seq_len // tm
    has_prev = i > 0
    has_next = i < n_tiles - 1
    sub = min(MIX_SUB, tm)
    n_ext = sub + 2 * HALO
    mid = slice(HALO, HALO + sub)
    cw = cw_ref[...]

    def ext(ref, col0, width, prev_ref, next_ref, r0):
        cols = slice(col0, col0 + width)
        if r0 == 0:
            before = jnp.where(has_prev, prev_ref[0].astype(F32), 0.0)
        else:
            before = ref[0, r0 - HALO:r0, cols].astype(F32)
        if r0 + sub == tm:
            after = jnp.where(has_next, next_ref[0].astype(F32), 0.0)
        else:
            after = ref[0, r0 + sub:r0 + sub + HALO, cols].astype(F32)
        return jnp.concatenate([before, ref[0, r0:r0 + sub, cols].astype(F32), after], axis=0)

    pace = None
    for r0 in range(0, tm, sub):
        rs = slice(r0, r0 + sub)
        v = ext(pc_ref, 2 * D_CONV, D_CONV, ccp_ref, ccn_ref, r0) * ext(pc_ref, 0, D_CONV, cup_ref, cun_ref, r0)
        u_ext = ext(pp_ref, 0, D_POOL, pup_ref, pun_ref, r0)
        if pace is not None:
            v = v + pace
            u_ext = u_ext + pace
        y = pltpu.roll(v, 1, 0)[mid] * cw[0:1] + v[mid] * cw[1:2] + pltpu.roll(v, n_ext - 1, 0)[mid] * cw[2:3]
        cb = pc_ref[0, rs, D_CONV:2 * D_CONV].astype(F32)
        cg = pc_ref[0, rs, 3 * D_CONV:4 * D_CONV].astype(F32)
        mix_ref[rs, 0:D_CONV] = (_silu(cg) * (cb * y)).astype(BF16)

        mix_ref[rs, D_CONV:D_CONV + D_ATTN] = ya_ref[0, rs, :]

        u = u_ext[mid]
        pg = pp_ref[0, rs, D_POOL:2 * D_POOL].astype(F32)
        t = i * tm + r0 + lax.broadcasted_iota(jnp.int32, (sub, 1), 0)
        for gi, w in enumerate(POOL_WINDOWS):
            sl = slice(gi * POOL_GROUP, (gi + 1) * POOL_GROUP)
            s = u_ext[:, sl]
            k = 1
            while k < w:
                s = s + pltpu.roll(s, n_ext - k, 0)
                k *= 2
            win = pltpu.roll(s, w // 2, 0)[mid]
            lo = jnp.maximum(t - w // 2, 0)
            hi = jnp.minimum(t + w - w // 2, seq_len)
            d = (win / (hi - lo).astype(F32) - u[:, sl]).astype(BF16)
            yg = jnp.dot(d, pw_ref[gi], preferred_element_type=F32)
            y_pool = _silu(pg[:, sl]) * (yg * ps_ref[:, sl])
            mix_ref[rs, D_CONV + D_ATTN + gi * POOL_GROUP:D_CONV + D_ATTN + (gi + 1) * POOL_GROUP] = (
                y_pool.astype(BF16))

        out = jnp.dot(mix_ref[rs, :], wo_ref[...], preferred_element_type=F32)
        pace = jnp.where(out[0:1, :D_CONV] < -jnp.inf, 1.0, 0.0)
        z = DEEPNORM_ALPHA * x_ref[0, rs, :] + g_ref[0] * out
        mu = jnp.mean(z, axis=-1, keepdims=True)
        zc = z - mu
        var = jnp.mean(zc * zc, axis=-1, keepdims=True)
        o_ref[0, rs, :] = zc * lax.rsqrt(var + LN_EPS) * lg_ref[...] + lb_ref[...]


def _mix(p, y_attn, x, mods3, mod_row, layer, conv_w, pool_w_bf16, pool_scale, w_out_bf16, ln_g, ln_b, tm):
    nb, t, _ = x.shape
    hb = tm // HALO
    last = t // HALO - 1
    prev = lambda i: jnp.maximum(i * hb - 1, 0)
    nxt = lambda i: jnp.minimum((i + 1) * hb, last)
    cc_blk = 2
    const2 = lambda b, i: (0, 0)
    in_specs = [
        pl.BlockSpec((1, tm, 4 * D_CONV), lambda b, i: (b, i, 0)),
        pl.BlockSpec((1, HALO, D_CONV), lambda b, i: (b, prev(i), 0)),
        pl.BlockSpec((1, HALO, D_CONV), lambda b, i: (b, prev(i), cc_blk)),
        pl.BlockSpec((1, HALO, D_CONV), lambda b, i: (b, nxt(i), 0)),
        pl.BlockSpec((1, HALO, D_CONV), lambda b, i: (b, nxt(i), cc_blk)),
        pl.BlockSpec((1, tm, 2 * D_POOL), lambda b, i: (b, i, POOL_OFF // (2 * D_POOL))),
        pl.BlockSpec((1, HALO, D_POOL), lambda b, i: (b, prev(i), POOL_OFF // D_POOL)),
        pl.BlockSpec((1, HALO, D_POOL), lambda b, i: (b, nxt(i), POOL_OFF // D_POOL)),
        pl.BlockSpec((1, tm, D_ATTN), lambda b, i: (b, i, 0)),
        pl.BlockSpec((1, tm, D_MODEL), lambda b, i: (b, i, 0)),
        pl.BlockSpec((1, 1, D_MODEL), lambda b, i: (mod_row(b), 0, 2)),
        pl.BlockSpec(conv_w.shape, const2),
        pl.BlockSpec((None,) + pool_w_bf16.shape[1:], lambda b, i: (layer, 0, 0, 0)),
        pl.BlockSpec(pool_scale.shape, const2),
        pl.BlockSpec((None,) + w_out_bf16.shape[1:], lambda b, i: (layer, 0, 0), pipeline_mode=pl.Buffered(1)),
        pl.BlockSpec(ln_g.shape, const2),
        pl.BlockSpec(ln_b.shape, const2),
    ]
    return pl.pallas_call(
        functools.partial(_mix_kernel, seq_len=t),
        grid=(nb, t // tm),
        in_specs=in_specs,
        out_specs=pl.BlockSpec((1, tm, D_MODEL), lambda b, i: (b, i, 0)),
        out_shape=jax.ShapeDtypeStruct((nb, t, D_MODEL), F32),
        scratch_shapes=[pltpu.VMEM((tm, D_MODEL), BF16)],
        compiler_params=_params("arbitrary", "arbitrary"),
        name="mix_out_ln",
    )(p, p, p, p, p, p, p, p, y_attn, x, mods3, conv_w, pool_w_bf16, pool_scale, w_out_bf16, ln_g, ln_b)


def _rope_tables(seq_len):
    n = QK_DIM // 4
    inv = ROPE_THETA ** (-np.arange(n, dtype=np.float64) / n)
    tok = np.arange(seq_len)
    pos = np.stack([tok // GRID_W, tok % GRID_W], axis=1).astype(np.float64)
    m = np.arange(LANES)
    ang = pos[:, (m // n) % 2] * inv[m % n][None, :]
    second = (m >= LANES // 2)[None, :]
    cos, sin = np.cos(ang), np.sin(ang)
    return jnp.asarray(cos, dtype=F32), jnp.asarray(np.where(second, sin, -sin), dtype=F32)


def kernel(x, c, ctx, c_ctx, w_mod, b_mod, w_in, conv_w, lam_q1, lam_k1, lam_q2, lam_k2, subln_g, pool_w,
           pool_scale, w_out, ln_g, ln_b):
    nb, seq, _ = x.shape
    ctx_len = ctx.shape[1]
    assert nb + 1 <= MOD_ROWS

    cvec = jnp.concatenate([c, c_ctx[None, :], jnp.zeros((MOD_ROWS - nb - 1, D_MODEL), F32)], axis=0)
    lamv = jnp.stack([lam_q1, lam_k1, lam_q2, lam_k2], axis=1).astype(F32)
    mods, lams = _modulation(cvec, w_mod, b_mod, lamv)
    tables = _rope_tables(seq)
    latent_row = lambda b: b
    ctx_row = lambda b: nb

    w_in_bf = _head_layout(w_in.astype(BF16))
    w_out_bf = w_out.astype(BF16)
    pool_w_bf = pool_w.astype(BF16)

    xl, xc = x, ctx
    for l in range(DEPTH):
        last = l == DEPTH - 1
        lam_init = 0.8 - 0.6 * math.exp(-0.3 * l)
        mods3 = mods[l].reshape(MOD_ROWS, 1, 3 * D_MODEL)
        gain = subln_g[l].reshape(1, V_DIM)
        p_l = _project(xl, mods3, latent_row, w_in_bf, l, 0, D_IN // PROJ_TN, PROJ_TM, tables)
        xc_flat = xc.reshape(1, nb * ctx_len, D_MODEL)
        if last:
            p_c = _project(xc_flat, mods3, ctx_row, w_in_bf, l, K_OFF // PROJ_TN, 2 * D_ATTN // PROJ_TN,
                           nb * ctx_len, None).reshape(nb, ctx_len, 2 * D_ATTN)
            kc_off, vc_off = 0, D_ATTN
        else:
            p_c = _project(xc_flat, mods3, ctx_row, w_in_bf, l, 0, D_IN // PROJ_TN, nb * ctx_len,
                           None).reshape(nb, ctx_len, D_IN)
            kc_off, vc_off = K_OFF, V_OFF
        y_attn = _attention(lams[l], gain, p_l, p_c, kc_off, vc_off, ATTN_TQ, lam_init)
        mix_args = (l, conv_w[l], pool_w_bf, pool_scale[l].reshape(1, D_POOL), w_out_bf,
                    ln_g[l].reshape(1, D_MODEL), ln_b[l].reshape(1, D_MODEL))
        xl_new = _mix(p_l, y_attn, xl, mods3, latent_row, *mix_args, MIX_TM)
        if not last:
            y_attn_c = _attention_ctx(lams[l], gain, p_c, lam_init)
            xc = _mix(p_c, y_attn_c, xc, mods3, ctx_row, *mix_args, ctx_len)
        xl = xl_new
    return xl
```

```python
import functools
import math

import numpy as np
import jax
import jax.numpy as jnp
from jax import lax
from jax.experimental import pallas as pl
from jax.experimental.pallas import tpu as pltpu

F32 = jnp.float32
BF16 = jnp.bfloat16

D_MODEL = 2048
DEPTH = 2
GRID_W = 64
D_CONV = 512
D_ATTN = 1024
D_POOL = 512
N_HEADS = 8
V_DIM = 128
QK_DIM = 64
ATTN_SCALE = QK_DIM ** -0.5
LOG2_E = math.log2(math.e)
Q_PRESCALE = ATTN_SCALE * LOG2_E
POOL_WINDOWS = (2, 4, 8, 16)
POOL_GROUP = 128
ROPE_THETA = 10000.0
LN_EPS = 1e-5
RMS_EPS = 1e-5
DEEPNORM_ALPHA = (2 * DEPTH) ** 0.25
D_IN = 4 * D_CONV + 4 * D_ATTN + 2 * D_POOL
Q_OFF = 4 * D_CONV
K_OFF = Q_OFF + D_ATTN
V_OFF = K_OFF + D_ATTN
G_OFF = V_OFF + D_ATTN
POOL_OFF = G_OFF + D_ATTN

LANES = 128
SUBLANES = 8
BF16_ROWS = 16
HALO = BF16_ROWS
MOD_ROWS = 8
VMEM_LIMIT = 56 * 1024 * 1024

PROJ_TM = 1024
PROJ_TN = 1024
ATTN_TQ = 256
ATTN_CHUNK = 256
MIX_TM = 512
MIX_SUB = 256
MOD_TN = 1536


def _silu(x):
    return x / (1.0 + jnp.exp(-x))


def _params(*sem):
    return pltpu.CompilerParams(dimension_semantics=sem, vmem_limit_bytes=VMEM_LIMIT)


def _mod_kernel(c_ref, w_ref, b_ref, lamv_ref, mod_ref, lam_ref):
    layer = pl.program_id(0)
    s = _silu(c_ref[...]).astype(BF16)
    w = w_ref[0].astype(BF16)
    mod_ref[0] = jnp.dot(s, w, preferred_element_type=F32) + b_ref[0]
    lv = lamv_ref[0]
    d1 = jnp.sum(lv[0:1] * lv[1:2], axis=-1, keepdims=True)
    d2 = jnp.sum(lv[2:3] * lv[3:4], axis=-1, keepdims=True)
    lf = jnp.zeros((1, LANES), F32) + layer.astype(F32)
    lam_init = 0.8 - 0.6 * jnp.exp(-0.3 * lf)
    lam_ref[0] = jnp.exp(d1) - jnp.exp(d2) + lam_init


def _modulation(cvec, w_mod, b_mod, lamv):
    n_out = w_mod.shape[-1]
    return pl.pallas_call(
        _mod_kernel,
        grid=(DEPTH, n_out // MOD_TN),
        in_specs=[
            pl.BlockSpec((MOD_ROWS, D_MODEL), lambda l, j: (0, 0)),
            pl.BlockSpec((1, D_MODEL, MOD_TN), lambda l, j: (l, 0, j)),
            pl.BlockSpec((1, 1, MOD_TN), lambda l, j: (l, 0, j)),
            pl.BlockSpec((1, 4, QK_DIM), lambda l, j: (l, 0, 0)),
        ],
        out_specs=[
            pl.BlockSpec((1, MOD_ROWS, MOD_TN), lambda l, j: (l, 0, j)),
            pl.BlockSpec((1, 1, LANES), lambda l, j: (l, 0, 0)),
        ],
        out_shape=[
            jax.ShapeDtypeStruct((DEPTH, MOD_ROWS, n_out), F32),
            jax.ShapeDtypeStruct((DEPTH, 1, LANES), F32),
        ],
        compiler_params=_params("arbitrary", "arbitrary"),
        name="modulation",
    )(cvec, w_mod, b_mod.reshape(DEPTH, 1, n_out), lamv)


def _proj_kernel(*refs, rope, q_blk, k_blk):
    if rope:
        x_ref, sh_ref, sc_ref, w_ref, cos_ref, sin_ref, o_ref, h_ref = refs
    else:
        x_ref, sh_ref, sc_ref, w_ref, o_ref, h_ref = refs
    j = pl.program_id(2)

    @pl.when(j == 0)
    def _():
        h_ref[...] = (x_ref[0] * (1.0 + sc_ref[0]) + sh_ref[0]).astype(BF16)

    tm, tn = o_ref.shape[1], o_ref.shape[2]
    half = tm // 2
    blocks = (slice(0, half), slice(half, tm))

    def matmul_blocks():
        accs, pace = [], None
        for rs in blocks:
            lhs = h_ref[rs, :]
            if pace is not None:
                lhs = lhs + pace
            acc = jnp.dot(lhs, w_ref[...], preferred_element_type=F32)
            zero = jnp.where(acc[0:1, :] < -jnp.inf, 1.0, 0.0).astype(BF16)
            pace = jnp.concatenate([zero] * (h_ref.shape[1] // tn), axis=1)
            accs.append(acc)
        return accs

    def plain():
        for rs, acc in zip(blocks, matmul_blocks()):
            o_ref[0, rs, :] = acc.astype(BF16)

    if not rope:
        plain()
    else:
        is_rope = (j == q_blk) | (j == k_blk)
        pl.when(jnp.logical_not(is_rope))(plain)

        @pl.when(is_rope)
        def _():
            scale = jnp.where(j == q_blk, Q_PRESCALE, 1.0).astype(F32)
            for rs, acc in zip(blocks, matmul_blocks()):
                cos = cos_ref[rs, :] * scale
                sin = sin_ref[rs, :] * scale
                for c in range(tn // LANES):
                    xc = acc[:, c * LANES:(c + 1) * LANES]
                    r = xc * cos + pltpu.roll(xc, LANES // 2, 1) * sin
                    o_ref[0, rs, c * LANES:(c + 1) * LANES] = r.astype(BF16)


def _project(x, mods3, mod_row, w_bf16, layer, col_blk0, n_col_blks, tm, rope_tables):
    nb, t, _ = x.shape
    tn = PROJ_TN
    rope = rope_tables is not None
    in_specs = [
        pl.BlockSpec((1, tm, D_MODEL), lambda b, i, j: (b, i, 0)),
        pl.BlockSpec((1, 1, D_MODEL), lambda b, i, j: (mod_row(b), 0, 0)),
        pl.BlockSpec((1, 1, D_MODEL), lambda b, i, j: (mod_row(b), 0, 1)),
        pl.BlockSpec((None, D_MODEL, tn), lambda b, i, j: (layer, 0, j + col_blk0)),
    ]
    args = [x, mods3, mods3, w_bf16]
    if rope:
        in_specs += [pl.BlockSpec((tm, LANES), lambda b, i, j: (i, 0))] * 2
        args += list(rope_tables)
    return pl.pallas_call(
        functools.partial(_proj_kernel, rope=rope, q_blk=Q_OFF // tn - col_blk0, k_blk=K_OFF // tn - col_blk0),
        grid=(nb, t // tm, n_col_blks),
        in_specs=in_specs,
        out_specs=pl.BlockSpec((1, tm, tn), lambda b, i, j: (b, i, j)),
        out_shape=jax.ShapeDtypeStruct((nb, t, n_col_blks * tn), BF16),
        scratch_shapes=[pltpu.VMEM((tm, D_MODEL), BF16)],
        compiler_params=_params("arbitrary", "arbitrary", "arbitrary"),
        name="proj_rope" if rope else "proj",
    )(*args)


def _head_layout(w):
    n_layers, d, _ = w.shape
    n_freq = QK_DIM // 4
    qk = w[:, :, Q_OFF:V_OFF].reshape(n_layers, d, 2, N_HEADS, 2, 2, 2, n_freq)
    qk = qk.transpose(0, 1, 2, 3, 6, 4, 5, 7).reshape(n_layers, d, 2 * D_ATTN)
    return lax.dynamic_update_slice(w, qk, (0, 0, Q_OFF))


def _split_halves(q):
    lane = lax.broadcasted_iota(jnp.int32, (1, LANES), 1)
    first = (lane // (QK_DIM // 2)) % 2 == 0
    zero = jnp.zeros_like(q)
    return jnp.concatenate([jnp.where(first, q, zero), jnp.where(first, zero, q)], axis=0)


def _attn_ctx_kernel(lam_ref, gain_ref, q_ref, ag_ref, k_ref, v_ref, o_ref, *, lam_init):
    tq = q_ref.shape[1]
    for h in range(N_HEADS):
        cols = slice(h * V_DIM, (h + 1) * V_DIM)
        qs = _split_halves(q_ref[0, :, cols] * ATTN_SCALE)
        s = lax.dot_general(qs, k_ref[0, :, cols], (((1,), (1,)), ((), ())), preferred_element_type=F32)
        p = jnp.exp(s - s.max(axis=-1, keepdims=True))
        den = p.sum(axis=-1, keepdims=True)
        den1, den2 = den[:tq], den[tq:]
        ratio = lam_ref[:, 0:1] * den1 / den2
        a = (p[:tq] - ratio * p[tq:]).astype(BF16)
        o = jnp.dot(a, v_ref[0, :, cols], preferred_element_type=F32) / den1
        y = o * lax.rsqrt(jnp.mean(o * o, axis=-1, keepdims=True) + RMS_EPS) * (gain_ref[...] * (1.0 - lam_init))
        o_ref[0, :, cols] = (_silu(ag_ref[0, :, cols].astype(F32)) * y).astype(BF16)


def _attention_ctx(lam, gain, p_c, lam_init):
    nb, t, _ = p_c.shape
    col = lambda off: (lambda b: (b, 0, off // D_ATTN))
    return pl.pallas_call(
        functools.partial(_attn_ctx_kernel, lam_init=lam_init),
        grid=(nb,),
        in_specs=[
            pl.BlockSpec((1, LANES), lambda b: (0, 0)),
            pl.BlockSpec((1, V_DIM), lambda b: (0, 0)),
            pl.BlockSpec((1, t, D_ATTN), col(Q_OFF)),
            pl.BlockSpec((1, t, D_ATTN), col(G_OFF)),
            pl.BlockSpec((1, t, D_ATTN), col(K_OFF)),
            pl.BlockSpec((1, t, D_ATTN), col(V_OFF)),
        ],
        out_specs=pl.BlockSpec((1, t, D_ATTN), col(0)),
        out_shape=jax.ShapeDtypeStruct((nb, t, D_ATTN), BF16),
        compiler_params=_params("arbitrary"),
        name="diff_attn_ctx",
    )(lam, gain, p_c, p_c, p_c, p_c)


def _attn_kernel(lam_ref, gain_ref, q_ref, ag_ref, kc_ref, kl_ref, vc_ref, vl_ref, o_ref,
                 s0, s1, m0, m1, p0, p1, l0, l1, *, lam_init):
    step = pl.program_id(0)
    tq = q_ref.shape[1] // 2
    nt = (((1,), (1,)), ((), ()))
    tn = (((0,), (0,)), ((), ()))
    chunks = ([(kc_ref, vc_ref, r) for r in range(0, kc_ref.shape[1], ATTN_CHUNK)]
              + [(kl_ref, vl_ref, r) for r in range(0, kl_ref.shape[1], ATTN_CHUNK)])
    groups = ATTN_CHUNK // SUBLANES

    @pl.when(step == 0)
    def _():
        s1[...] = jnp.zeros_like(s1)
        m1[...] = jnp.zeros_like(m1)
        p0[...] = jnp.zeros_like(p0)
        l0[...] = jnp.ones_like(l0)

    def half(tile_rows, s_w, m_w, s_r, m_r, p_w, l_w, p_r, l_r):
        qs = _split_halves(q_ref[0, tile_rows, :])
        m_prev = m_r[...]
        den = l_r[...]
        den1, den2 = den[:, :tq], den[:, tq:]
        ratio = (lam_ref[:, 0:1] * den1 / den2).astype(BF16)
        m_acc = jnp.full((SUBLANES, 2 * tq), -jnp.inf, F32)
        l_acc = jnp.zeros((SUBLANES, 2 * tq), F32)
        o_t = jnp.zeros((V_DIM, tq), F32)
        for c, (k_ref, v_ref, r0) in enumerate(chunks):
            rows = slice(c * ATTN_CHUNK, (c + 1) * ATTN_CHUNK)
            s = lax.dot_general(k_ref[0, r0:r0 + ATTN_CHUNK, :], qs, nt, preferred_element_type=F32)
            s_w[rows, :] = s
            pace = jnp.where(l_acc < 0.0, 1.0, 0.0)
            m_acc = jnp.maximum(m_acc + pace, s.reshape(groups, SUBLANES, 2 * tq).max(axis=0))
            p = jnp.exp2(s_r[rows, :] - m_prev)
            l_acc = l_acc + p.reshape(groups, SUBLANES, 2 * tq).sum(axis=0)
            p_w[rows, :] = p.astype(BF16)
            pr = p_r[rows, :]
            a = pr[:, :tq] - ratio * pr[:, tq:]
            o_t = o_t + lax.dot_general(v_ref[0, r0:r0 + ATTN_CHUNK, :], a, tn, preferred_element_type=F32)
        m_w[...] = m_acc.max(axis=0, keepdims=True)
        l_w[...] = l_acc.sum(axis=0, keepdims=True)
        o = (o_t * (1.0 / den1)).T
        y = o * lax.rsqrt(jnp.mean(o * o, axis=-1, keepdims=True) + RMS_EPS) * (gain_ref[...] * (1.0 - lam_init))
        o_ref[0, tile_rows, :] = (_silu(ag_ref[0, tile_rows, :].astype(F32)) * y).astype(BF16)

    half(slice(0, tq), s0, m0, s1, m1, p1, l1, p0, l0)
    half(slice(tq, 2 * tq), s1, m1, s0, m0, p0, l0, p1, l1)


def _attention(lam, gain, p_l, p_c, kc_off, vc_off, tq, lam_init):
    nb, seq, _ = p_l.shape
    n_ctx = p_c.shape[1]
    assert n_ctx % ATTN_CHUNK == 0 and seq % ATTN_CHUNK == 0 and seq % (2 * tq) == 0
    n_keys = n_ctx + seq
    pairs_per_head = seq // (2 * tq)
    n_pairs = nb * N_HEADS * pairs_per_head

    def pair(k):
        k = jnp.clip(k, 0, n_pairs - 1)
        return k // (N_HEADS * pairs_per_head), (k // pairs_per_head) % N_HEADS, k % pairs_per_head

    def tile_map(off, lag):
        def f(k):
            b, h, i = pair(k - lag)
            return b, i, off // V_DIM + h
        return f

    def head_map(off, lag):
        def f(k):
            b, h, _ = pair(k - lag)
            return b, 0, off // V_DIM + h
        return f

    return pl.pallas_call(
        functools.partial(_attn_kernel, lam_init=lam_init),
        grid=(n_pairs + 1,),
        in_specs=[
            pl.BlockSpec((1, LANES), lambda k: (0, 0)),
            pl.BlockSpec((1, V_DIM), lambda k: (0, 0)),
            pl.BlockSpec((1, 2 * tq, V_DIM), tile_map(Q_OFF, 0)),
            pl.BlockSpec((1, 2 * tq, V_DIM), tile_map(G_OFF, 1)),
            pl.BlockSpec((1, n_ctx, V_DIM), head_map(kc_off, 0)),
            pl.BlockSpec((1, seq, V_DIM), head_map(K_OFF, 0)),
            pl.BlockSpec((1, n_ctx, V_DIM), head_map(vc_off, 1)),
            pl.BlockSpec((1, seq, V_DIM), head_map(V_OFF, 1)),
        ],
        out_specs=pl.BlockSpec((1, 2 * tq, V_DIM), tile_map(0, 1)),
        out_shape=jax.ShapeDtypeStruct((nb, seq, D_ATTN), BF16),
        scratch_shapes=[
            pltpu.VMEM((n_keys, 2 * tq), F32), pltpu.VMEM((n_keys, 2 * tq), F32),
            pltpu.VMEM((1, 2 * tq), F32), pltpu.VMEM((1, 2 * tq), F32),
            pltpu.VMEM((n_keys, 2 * tq), BF16), pltpu.VMEM((n_keys, 2 * tq), BF16),
            pltpu.VMEM((1, 2 * tq), F32), pltpu.VMEM((1, 2 * tq), F32),
        ],
        compiler_params=_params("arbitrary"),
        name="diff_attn",
    )(lam, gain, p_l, p_l, p_c, p_l, p_c, p_l)


def _mix_kernel(pc_ref, cup_ref, ccp_ref, cun_ref, ccn_ref, pp_ref, pup_ref, pun_ref, ya_ref, x_ref, g_ref,
                cw_ref, pw_ref, ps_ref, wo_ref, lg_ref, lb_ref, o_ref, mix_ref, *, seq_len):
    i = pl.program_id(1)
    tm = x_ref.shape[1]
    n_tiles = seq_len // tm
    has_prev = i > 0
    has_next = i < n_tiles - 1
    sub = min(MIX_SUB, tm)
    n_ext = sub + 2 * HALO
    mid = slice(HALO, HALO + sub)
    cw = cw_ref[...]

    def ext(ref, col0, width, prev_ref, next_ref, r0):
        cols = slice(col0, col0 + width)
        if r0 == 0:
            before = jnp.where(has_prev, prev_ref[0].astype(F32), 0.0)
        else:
            before = ref[0, r0 - HALO:r0, cols].astype(F32)
        if r0 + sub == tm:
            after = jnp.where(has_next, next_ref[0].astype(F32), 0.0)
        else:
            after = ref[0, r0 + sub:r0 + sub + HALO, cols].astype(F32)
        return jnp.concatenate([before, ref[0, r0:r0 + sub, cols].astype(F32), after], axis=0)

    pace = None
    for r0 in range(0, tm, sub):
        rs = slice(r0, r0 + sub)
        v = ext(pc_ref, 2 * D_CONV, D_CONV, ccp_ref, ccn_ref, r0) * ext(pc_ref, 0, D_CONV, cup_ref, cun_ref, r0)
        u_ext = ext(pp_ref, 0, D_POOL, pup_ref, pun_ref, r0)
        if pace is not None:
            v = v + pace
            u_ext = u_ext + pace
        y = pltpu.roll(v, 1, 0)[mid] * cw[0:1] + v[mid] * cw[1:2] + pltpu.roll(v, n_ext - 1, 0)[mid] * cw[2:3]
        cb = pc_ref[0, rs, D_CONV:2 * D_CONV].astype(F32)
        cg = pc_ref[0, rs, 3 * D_CONV:4 * D_CONV].astype(F32)
        mix_ref[rs, 0:D_CONV] = (_silu(cg) * (cb * y)).astype(BF16)

        mix_ref[rs, D_CONV:D_CONV + D_ATTN] = ya_ref[0, rs, :]

        u = u_ext[mid]
        pg = pp_ref[0, rs, D_POOL:2 * D_POOL].astype(F32)
        t = i * tm + r0 + lax.broadcasted_iota(jnp.int32, (sub, 1), 0)
        for gi, w in enumerate(POOL_WINDOWS):
            sl = slice(gi * POOL_GROUP, (gi + 1) * POOL_GROUP)
            s = u_ext[:, sl]
            k = 1
            while k < w:
                s = s + pltpu.roll(s, n_ext - k, 0)
                k *= 2
            win = pltpu.roll(s, w // 2, 0)[mid]
            lo = jnp.maximum(t - w // 2, 0)
            hi = jnp.minimum(t + w - w // 2, seq_len)
            d = (win / (hi - lo).astype(F32) - u[:, sl]).astype(BF16)
            yg = jnp.dot(d, pw_ref[gi], preferred_element_type=F32)
            y_pool = _silu(pg[:, sl]) * (yg * ps_ref[:, sl])
            mix_ref[rs, D_CONV + D_ATTN + gi * POOL_GROUP:D_CONV + D_ATTN + (gi + 1) * POOL_GROUP] = (
                y_pool.astype(BF16))

        out = jnp.dot(mix_ref[rs, :], wo_ref[...], preferred_element_type=F32)
        pace = jnp.where(out[0:1, :D_CONV] < -jnp.inf, 1.0, 0.0)
        z = DEEPNORM_ALPHA * x_ref[0, rs, :] + g_ref[0] * out
        mu = jnp.mean(z, axis=-1, keepdims=True)
        zc = z - mu
        var = jnp.mean(zc * zc, axis=-1, keepdims=True)
        o_ref[0, rs, :] = zc * lax.rsqrt(var + LN_EPS) * lg_ref[...] + lb_ref[...]


def _mix(p, y_attn, x, mods3, mod_row, layer, conv_w, pool_w_bf16, pool_scale, w_out_bf16, ln_g, ln_b, tm):
    nb, t, _ = x.shape
    hb = tm // HALO
    last = t // HALO - 1
    prev = lambda i: jnp.maximum(i * hb - 1, 0)
    nxt = lambda i: jnp.minimum((i + 1) * hb, last)
    cc_blk = 2
    const2 = lambda b, i: (0, 0)
    in_specs = [
        pl.BlockSpec((1, tm, 4 * D_CONV), lambda b, i: (b, i, 0)),
        pl.BlockSpec((1, HALO, D_CONV), lambda b, i: (b, prev(i), 0)),
        pl.BlockSpec((1, HALO, D_CONV), lambda b, i: (b, prev(i), cc_blk)),
        pl.BlockSpec((1, HALO, D_CONV), lambda b, i: (b, nxt(i), 0)),
        pl.BlockSpec((1, HALO, D_CONV), lambda b, i: (b, nxt(i), cc_blk)),
        pl.BlockSpec((1, tm, 2 * D_POOL), lambda b, i: (b, i, POOL_OFF // (2 * D_POOL))),
        pl.BlockSpec((1, HALO, D_POOL), lambda b, i: (b, prev(i), POOL_OFF // D_POOL)),
        pl.BlockSpec((1, HALO, D_POOL), lambda b, i: (b, nxt(i), POOL_OFF // D_POOL)),
        pl.BlockSpec((1, tm, D_ATTN), lambda b, i: (b, i, 0)),
        pl.BlockSpec((1, tm, D_MODEL), lambda b, i: (b, i, 0)),
        pl.BlockSpec((1, 1, D_MODEL), lambda b, i: (mod_row(b), 0, 2)),
        pl.BlockSpec(conv_w.shape, const2),
        pl.BlockSpec((None,) + pool_w_bf16.shape[1:], lambda b, i: (layer, 0, 0, 0)),
        pl.BlockSpec(pool_scale.shape, const2),
        pl.BlockSpec((None,) + w_out_bf16.shape[1:], lambda b, i: (layer, 0, 0), pipeline_mode=pl.Buffered(1)),
        pl.BlockSpec(ln_g.shape, const2),
        pl.BlockSpec(ln_b.shape, const2),
    ]
    return pl.pallas_call(
        functools.partial(_mix_kernel, seq_len=t),
        grid=(nb, t // tm),
        in_specs=in_specs,
        out_specs=pl.BlockSpec((1, tm, D_MODEL), lambda b, i: (b, i, 0)),
        out_shape=jax.ShapeDtypeStruct((nb, t, D_MODEL), F32),
        scratch_shapes=[pltpu.VMEM((tm, D_MODEL), BF16)],
        compiler_params=_params("arbitrary", "arbitrary"),
        name="mix_out_ln",
    )(p, p, p, p, p, p, p, p, y_attn, x, mods3, conv_w, pool_w_bf16, pool_scale, w_out_bf16, ln_g, ln_b)


def _rope_tables(seq_len):
    n = QK_DIM // 4
    inv = ROPE_THETA ** (-np.arange(n, dtype=np.float64) / n)
    tok = np.arange(seq_len)
    pos = np.stack([tok // GRID_W, tok % GRID_W], axis=1).astype(np.float64)
    m = np.arange(LANES)
    ang = pos[:, (m // n) % 2] * inv[m % n][None, :]
    second = (m >= LANES // 2)[None, :]
    cos, sin = np.cos(ang), np.sin(ang)
    return jnp.asarray(cos, dtype=F32), jnp.asarray(np.where(second, sin, -sin), dtype=F32)


def kernel(x, c, ctx, c_ctx, w_mod, b_mod, w_in, conv_w, lam_q1, lam_k1, lam_q2, lam_k2, subln_g, pool_w,
           pool_scale, w_out, ln_g, ln_b):
    nb, seq, _ = x.shape
    ctx_len = ctx.shape[1]
    assert nb + 1 <= MOD_ROWS

    cvec = jnp.concatenate([c, c_ctx[None, :], jnp.zeros((MOD_ROWS - nb - 1, D_MODEL), F32)], axis=0)
    lamv = jnp.stack([lam_q1, lam_k1, lam_q2, lam_k2], axis=1).astype(F32)
    mods, lams = _modulation(cvec, w_mod, b_mod, lamv)
    tables = _rope_tables(seq)
    latent_row = lambda b: b
    ctx_row = lambda b: nb

    w_in_bf = _head_layout(w_in.astype(BF16))
    w_out_bf = w_out.astype(BF16)
    pool_w_bf = pool_w.astype(BF16)

    xl, xc = x, ctx
    for l in range(DEPTH):
        last = l == DEPTH - 1
        lam_init = 0.8 - 0.6 * math.exp(-0.3 * l)
        mods3 = mods[l].reshape(MOD_ROWS, 1, 3 * D_MODEL)
        gain = subln_g[l].reshape(1, V_DIM)
        p_l = _project(xl, mods3, latent_row, w_in_bf, l, 0, D_IN // PROJ_TN, PROJ_TM, tables)
        xc_flat = xc.reshape(1, nb * ctx_len, D_MODEL)
        if last:
            p_c = _project(xc_flat, mods3, ctx_row, w_in_bf, l, K_OFF // PROJ_TN, 2 * D_ATTN // PROJ_TN,
                           nb * ctx_len, None).reshape(nb, ctx_len, 2 * D_ATTN)
            kc_off, vc_off = 0, D_ATTN
        else:
            p_c = _project(xc_flat, mods3, ctx_row, w_in_bf, l, 0, D_IN // PROJ_TN, nb * ctx_len,
                           None).reshape(nb, ctx_len, D_IN)
            kc_off, vc_off = K_OFF, V_OFF
        y_attn = _attention(lams[l], gain, p_l, p_c, kc_off, vc_off, ATTN_TQ, lam_init)
        mix_args = (l, conv_w[l], pool_w_bf, pool_scale[l].reshape(1, D_POOL), w_out_bf,
                    ln_g[l].reshape(1, D_MODEL), ln_b[l].reshape(1, D_MODEL))
        xl_new = _mix(p_l, y_attn, xl, mods3, latent_row, *mix_args, MIX_TM)
        if not last:
            y_attn_c = _attention_ctx(lams[l], gain, p_c, lam_init)
            xc = _mix(p_c, y_attn_c, xc, mods3, ctx_row, *mix_args, ctx_len)
        xl = xl_new
    return xl
```

```python
import functools
import math

import numpy as np
import jax
import jax.numpy as jnp
from jax import lax
from jax.experimental import pallas as pl
from jax.experimental.pallas import tpu as pltpu

F32 = jnp.float32
BF16 = jnp.bfloat16

D_MODEL = 2048
DEPTH = 2
GRID_W = 64
D_CONV = 512
D_ATTN = 1024
D_POOL = 512
N_HEADS = 8
V_DIM = 128
QK_DIM = 64
ATTN_SCALE = QK_DIM ** -0.5
LOG2_E = math.log2(math.e)
Q_PRESCALE = ATTN_SCALE * LOG2_E
POOL_WINDOWS = (2, 4, 8, 16)
POOL_GROUP = 128
ROPE_THETA = 10000.0
LN_EPS = 1e-5
RMS_EPS = 1e-5
DEEPNORM_ALPHA = (2 * DEPTH) ** 0.25
D_IN = 4 * D_CONV + 4 * D_ATTN + 2 * D_POOL
Q_OFF = 4 * D_CONV
K_OFF = Q_OFF + D_ATTN
V_OFF = K_OFF + D_ATTN
G_OFF = V_OFF + D_ATTN
POOL_OFF = G_OFF + D_ATTN

LANES = 128
SUBLANES = 8
BF16_ROWS = 16
HALO = BF16_ROWS
MOD_ROWS = 8
VMEM_LIMIT = 56 * 1024 * 1024

PROJ_TM = 1024
PROJ_TN = 1024
ATTN_TQ = 256
ATTN_CHUNK = 256
MIX_TM = 512
MIX_SUB = 256
MOD_TN = 1536


def _silu(x):
    return x / (1.0 + jnp.exp(-x))


def _params(*sem):
    return pltpu.CompilerParams(dimension_semantics=sem, vmem_limit_bytes=VMEM_LIMIT)


def _mod_kernel(c_ref, w_ref, b_ref, lamv_ref, mod_ref, lam_ref):
    layer = pl.program_id(0)
    s = _silu(c_ref[...]).astype(BF16)
    w = w_ref[0].astype(BF16)
    mod_ref[0] = jnp.dot(s, w, preferred_element_type=F32) + b_ref[0]
    lv = lamv_ref[0]
    d1 = jnp.sum(lv[0:1] * lv[1:2], axis=-1, keepdims=True)
    d2 = jnp.sum(lv[2:3] * lv[3:4], axis=-1, keepdims=True)
    lf = jnp.zeros((1, LANES), F32) + layer.astype(F32)
    lam_init = 0.8 - 0.6 * jnp.exp(-0.3 * lf)
    lam_ref[0] = jnp.exp(d1) - jnp.exp(d2) + lam_init


def _modulation(cvec, w_mod, b_mod, lamv):
    n_out = w_mod.shape[-1]
    return pl.pallas_call(
        _mod_kernel,
        grid=(DEPTH, n_out // MOD_TN),
        in_specs=[
            pl.BlockSpec((MOD_ROWS, D_MODEL), lambda l, j: (0, 0)),
            pl.BlockSpec((1, D_MODEL, MOD_TN), lambda l, j: (l, 0, j)),
            pl.BlockSpec((1, 1, MOD_TN), lambda l, j: (l, 0, j)),
            pl.BlockSpec((1, 4, QK_DIM), lambda l, j: (l, 0, 0)),
        ],
        out_specs=[
            pl.BlockSpec((1, MOD_ROWS, MOD_TN), lambda l, j: (l, 0, j)),
            pl.BlockSpec((1, 1, LANES), lambda l, j: (l, 0, 0)),
        ],
        out_shape=[
            jax.ShapeDtypeStruct((DEPTH, MOD_ROWS, n_out), F32),
            jax.ShapeDtypeStruct((DEPTH, 1, LANES), F32),
        ],
        compiler_params=_params("arbitrary", "arbitrary"),
        name="modulation",
    )(cvec, w_mod, b_mod.reshape(DEPTH, 1, n_out), lamv)


def _proj_kernel(*refs, rope, q_blk, k_blk):
    if rope:
        x_ref, sh_ref, sc_ref, w_ref, cos_ref, sin_ref, o_ref = refs
    else:
        x_ref, sh_ref, sc_ref, w_ref, o_ref = refs
    j = pl.program_id(2)
    tm, tn = o_ref.shape[1], o_ref.shape[2]
    half = tm // 2
    blocks = (slice(0, half), slice(half, tm))

    def matmul_blocks():
        accs, pace = [], None
        for rs in blocks:
            lhs = (x_ref[0, rs, :] * (1.0 + sc_ref[0]) + sh_ref[0]).astype(BF16)
            if pace is not None:
                lhs = lhs + pace
            acc = jnp.dot(lhs, w_ref[...], preferred_element_type=F32)
            zero = jnp.where(acc[0:1, :] < -jnp.inf, 1.0, 0.0).astype(BF16)
            pace = jnp.concatenate([zero] * (x_ref.shape[2] // tn), axis=1)
            accs.append(acc)
        return accs

    def plain():
        for rs, acc in zip(blocks, matmul_blocks()):
            o_ref[0, rs, :] = acc.astype(BF16)

    if not rope:
        plain()
    else:
        is_rope = (j == q_blk) | (j == k_blk)
        pl.when(jnp.logical_not(is_rope))(plain)

        @pl.when(is_rope)
        def _():
            scale = jnp.where(j == q_blk, Q_PRESCALE, 1.0).astype(F32)
            for rs, acc in zip(blocks, matmul_blocks()):
                cos = cos_ref[rs, :] * scale
                sin = sin_ref[rs, :] * scale
                for c in range(tn // LANES):
                    xc = acc[:, c * LANES:(c + 1) * LANES]
                    r = xc * cos + pltpu.roll(xc, LANES // 2, 1) * sin
                    o_ref[0, rs, c * LANES:(c + 1) * LANES] = r.astype(BF16)


def _project(x, mods3, mod_row, w_bf16, layer, col_blk0, n_col_blks, tm, rope_tables):
    nb, t, _ = x.shape
    tn = PROJ_TN
    rope = rope_tables is not None
    in_specs = [
        pl.BlockSpec((1, tm, D_MODEL), lambda b, i, j: (b, i, 0)),
        pl.BlockSpec((1, 1, D_MODEL), lambda b, i, j: (mod_row(b), 0, 0)),
        pl.BlockSpec((1, 1, D_MODEL), lambda b, i, j: (mod_row(b), 0, 1)),
        pl.BlockSpec((None, D_MODEL, tn), lambda b, i, j: (layer, 0, j + col_blk0)),
    ]
    args = [x, mods3, mods3, w_bf16]
    if rope:
        in_specs += [pl.BlockSpec((tm, LANES), lambda b, i, j: (i, 0))] * 2
        args += list(rope_tables)
    return pl.pallas_call(
        functools.partial(_proj_kernel, rope=rope, q_blk=Q_OFF // tn - col_blk0, k_blk=K_OFF // tn - col_blk0),
        grid=(nb, t // tm, n_col_blks),
        in_specs=in_specs,
        out_specs=pl.BlockSpec((1, tm, tn), lambda b, i, j: (b, i, j)),
        out_shape=jax.ShapeDtypeStruct((nb, t, n_col_blks * tn), BF16),
        compiler_params=_params("arbitrary", "arbitrary", "arbitrary"),
        name="proj_rope" if rope else "proj",
    )(*args)


def _head_layout(w):
    n_layers, d, _ = w.shape
    n_freq = QK_DIM // 4
    qk = w[:, :, Q_OFF:V_OFF].reshape(n_layers, d, 2, N_HEADS, 2, 2, 2, n_freq)
    qk = qk.transpose(0, 1, 2, 3, 6, 4, 5, 7).reshape(n_layers, d, 2 * D_ATTN)
    return lax.dynamic_update_slice(w, qk, (0, 0, Q_OFF))


def _split_halves(q):
    lane = lax.broadcasted_iota(jnp.int32, (1, LANES), 1)
    first = (lane // (QK_DIM // 2)) % 2 == 0
    zero = jnp.zeros_like(q)
    return jnp.concatenate([jnp.where(first, q, zero), jnp.where(first, zero, q)], axis=0)


def _attn_ctx_kernel(lam_ref, gain_ref, q_ref, ag_ref, k_ref, v_ref, o_ref, *, lam_init):
    tq = q_ref.shape[1]
    for h in range(N_HEADS):
        cols = slice(h * V_DIM, (h + 1) * V_DIM)
        qs = _split_halves(q_ref[0, :, cols] * ATTN_SCALE)
        s = lax.dot_general(qs, k_ref[0, :, cols], (((1,), (1,)), ((), ())), preferred_element_type=F32)
        p = jnp.exp(s - s.max(axis=-1, keepdims=True))
        den = p.sum(axis=-1, keepdims=True)
        den1, den2 = den[:tq], den[tq:]
        ratio = lam_ref[:, 0:1] * den1 / den2
        a = (p[:tq] - ratio * p[tq:]).astype(BF16)
        o = jnp.dot(a, v_ref[0, :, cols], preferred_element_type=F32) / den1
        y = o * lax.rsqrt(jnp.mean(o * o, axis=-1, keepdims=True) + RMS_EPS) * (gain_ref[...] * (1.0 - lam_init))
        o_ref[0, :, cols] = (_silu(ag_ref[0, :, cols].astype(F32)) * y).astype(BF16)


def _attention_ctx(lam, gain, p_c, lam_init):
    nb, t, _ = p_c.shape
    col = lambda off: (lambda b: (b, 0, off // D_ATTN))
    return pl.pallas_call(
        functools.partial(_attn_ctx_kernel, lam_init=lam_init),
        grid=(nb,),
        in_specs=[
            pl.BlockSpec((1, LANES), lambda b: (0, 0)),
            pl.BlockSpec((1, V_DIM), lambda b: (0, 0)),
            pl.BlockSpec((1, t, D_ATTN), col(Q_OFF)),
            pl.BlockSpec((1, t, D_ATTN), col(G_OFF)),
            pl.BlockSpec((1, t, D_ATTN), col(K_OFF)),
            pl.BlockSpec((1, t, D_ATTN), col(V_OFF)),
        ],
        out_specs=pl.BlockSpec((1, t, D_ATTN), col(0)),
        out_shape=jax.ShapeDtypeStruct((nb, t, D_ATTN), BF16),
        compiler_params=_params("arbitrary"),
        name="diff_attn_ctx",
    )(lam, gain, p_c, p_c, p_c, p_c)


def _attn_kernel(lam_ref, gain_ref, q_ref, ag_ref, kc_ref, kl_ref, vc_ref, vl_ref, o_ref,
                 s0, s1, m0, m1, p0, p1, l0, l1, *, lam_init):
    step = pl.program_id(0)
    tq = q_ref.shape[1] // 2
    nt = (((1,), (1,)), ((), ()))
    tn = (((0,), (0,)), ((), ()))
    chunks = ([(kc_ref, vc_ref, r) for r in range(0, kc_ref.shape[1], ATTN_CHUNK)]
              + [(kl_ref, vl_ref, r) for r in range(0, kl_ref.shape[1], ATTN_CHUNK)])
    groups = ATTN_CHUNK // SUBLANES

    @pl.when(step == 0)
    def _():
        s1[...] = jnp.zeros_like(s1)
        m1[...] = jnp.zeros_like(m1)
        p0[...] = jnp.zeros_like(p0)
        l0[...] = jnp.ones_like(l0)

    def half(tile_rows, s_w, m_w, s_r, m_r, p_w, l_w, p_r, l_r):
        qs = _split_halves(q_ref[0, tile_rows, :])
        m_prev = m_r[...]
        den = l_r[...]
        den1, den2 = den[:, :tq], den[:, tq:]
        ratio = (lam_ref[:, 0:1] * den1 / den2).astype(BF16)
        m_acc = jnp.full((SUBLANES, 2 * tq), -jnp.inf, F32)
        l_acc = jnp.zeros((SUBLANES, 2 * tq), F32)
        o_t = jnp.zeros((V_DIM, tq), F32)
        scores = {id(r): lax.dot_general(r[0], qs, nt, preferred_element_type=F32) for r in (kc_ref, kl_ref)}
        for c, (k_ref, v_ref, r0) in enumerate(chunks):
            rows = slice(c * ATTN_CHUNK, (c + 1) * ATTN_CHUNK)
            s = scores[id(k_ref)][r0:r0 + ATTN_CHUNK]
            s_w[rows, :] = s
            pace = jnp.where(l_acc < 0.0, 1.0, 0.0)
            m_acc = jnp.maximum(m_acc + pace, s.reshape(groups, SUBLANES, 2 * tq).max(axis=0))
            p = jnp.exp2(s_r[rows, :] - m_prev)
            l_acc = l_acc + p.reshape(groups, SUBLANES, 2 * tq).sum(axis=0)
            p_w[rows, :] = p.astype(BF16)
            pr = p_r[rows, :]
            a = pr[:, :tq] - ratio * pr[:, tq:]
            o_t = o_t + lax.dot_general(v_ref[0, r0:r0 + ATTN_CHUNK, :], a, tn, preferred_element_type=F32)
        m_w[...] = m_acc.max(axis=0, keepdims=True)
        l_w[...] = l_acc.sum(axis=0, keepdims=True)
        o = (o_t * (1.0 / den1)).T
        y = o * lax.rsqrt(jnp.mean(o * o, axis=-1, keepdims=True) + RMS_EPS) * (gain_ref[...] * (1.0 - lam_init))
        o_ref[0, tile_rows, :] = (_silu(ag_ref[0, tile_rows, :].astype(F32)) * y).astype(BF16)

    half(slice(0, tq), s0, m0, s1, m1, p1, l1, p0, l0)
    half(slice(tq, 2 * tq), s1, m1, s0, m0, p0, l0, p1, l1)


def _attention(lam, gain, p_l, p_c, kc_off, vc_off, tq, lam_init):
    nb, seq, _ = p_l.shape
    n_ctx = p_c.shape[1]
    assert n_ctx % ATTN_CHUNK == 0 and seq % ATTN_CHUNK == 0 and seq % (2 * tq) == 0
    n_keys = n_ctx + seq
    pairs_per_head = seq // (2 * tq)
    n_pairs = nb * N_HEADS * pairs_per_head

    def pair(k):
        k = jnp.clip(k, 0, n_pairs - 1)
        return k // (N_HEADS * pairs_per_head), (k // pairs_per_head) % N_HEADS, k % pairs_per_head

    def tile_map(off, lag):
        def f(k):
            b, h, i = pair(k - lag)
            return b, i, off // V_DIM + h
        return f

    def head_map(off, lag):
        def f(k):
            b, h, _ = pair(k - lag)
            return b, 0, off // V_DIM + h
        return f

    return pl.pallas_call(
        functools.partial(_attn_kernel, lam_init=lam_init),
        grid=(n_pairs + 1,),
        in_specs=[
            pl.BlockSpec((1, LANES), lambda k: (0, 0)),
            pl.BlockSpec((1, V_DIM), lambda k: (0, 0)),
            pl.BlockSpec((1, 2 * tq, V_DIM), tile_map(Q_OFF, 0)),
            pl.BlockSpec((1, 2 * tq, V_DIM), tile_map(G_OFF, 1)),
            pl.BlockSpec((1, n_ctx, V_DIM), head_map(kc_off, 0)),
            pl.BlockSpec((1, seq, V_DIM), head_map(K_OFF, 0)),
            pl.BlockSpec((1, n_ctx, V_DIM), head_map(vc_off, 1)),
            pl.BlockSpec((1, seq, V_DIM), head_map(V_OFF, 1)),
        ],
        out_specs=pl.BlockSpec((1, 2 * tq, V_DIM), tile_map(0, 1)),
        out_shape=jax.ShapeDtypeStruct((nb, seq, D_ATTN), BF16),
        scratch_shapes=[
            pltpu.VMEM((n_keys, 2 * tq), F32), pltpu.VMEM((n_keys, 2 * tq), F32),
            pltpu.VMEM((1, 2 * tq), F32), pltpu.VMEM((1, 2 * tq), F32),
            pltpu.VMEM((n_keys, 2 * tq), BF16), pltpu.VMEM((n_keys, 2 * tq), BF16),
            pltpu.VMEM((1, 2 * tq), F32), pltpu.VMEM((1, 2 * tq), F32),
        ],
        compiler_params=_params("arbitrary"),
        name="diff_attn",
    )(lam, gain, p_l, p_l, p_c, p_l, p_c, p_l)


def _mix_kernel(pc_ref, cup_ref, ccp_ref, cun_ref, ccn_ref, pp_ref, pup_ref, pun_ref, ya_ref, x_ref, g_ref,
                cw_ref, pw_ref, ps_ref, wo_ref, lg_ref, lb_ref, o_ref, mix_ref, *, seq_len):
    i = pl.program_id(1)
    tm = x_ref.shape[1]
    n_tiles = seq_len // tm
    has_prev = i > 0
    has_next = i < n_tiles - 1
    sub = min(MIX_SUB, tm)
    n_ext = sub + 2 * HALO
    mid = slice(HALO, HALO + sub)
    cw = cw_ref[...]

    def ext(ref, col0, width, prev_ref, next_ref, r0):
        cols = slice(col0, col0 + width)
        if r0 == 0:
            before = jnp.where(has_prev, prev_ref[0].astype(F32), 0.0)
        else:
            before = ref[0, r0 - HALO:r0, cols].astype(F32)
        if r0 + sub == tm:
            after = jnp.where(has_next, next_ref[0].astype(F32), 0.0)
        else:
            after = ref[0, r0 + sub:r0 + sub + HALO, cols].astype(F32)
        return jnp.concatenate([before, ref[0, r0:r0 + sub, cols].astype(F32), after], axis=0)

    pace = None
    for r0 in range(0, tm, sub):
        rs = slice(r0, r0 + sub)
        v = ext(pc_ref, 2 * D_CONV, D_CONV, ccp_ref, ccn_ref, r0) * ext(pc_ref, 0, D_CONV, cup_ref, cun_ref, r0)
        u_ext = ext(pp_ref, 0, D_POOL, pup_ref, pun_ref, r0)
        if pace is not None:
            v = v + pace
            u_ext = u_ext + pace
        y = pltpu.roll(v, 1, 0)[mid] * cw[0:1] + v[mid] * cw[1:2] + pltpu.roll(v, n_ext - 1, 0)[mid] * cw[2:3]
        cb = pc_ref[0, rs, D_CONV:2 * D_CONV].astype(F32)
        cg = pc_ref[0, rs, 3 * D_CONV:4 * D_CONV].astype(F32)
        mix_ref[rs, 0:D_CONV] = (_silu(cg) * (cb * y)).astype(BF16)

        mix_ref[rs, D_CONV:D_CONV + D_ATTN] = ya_ref[0, rs, :]

        u = u_ext[mid]
        pg = pp_ref[0, rs, D_POOL:2 * D_POOL].astype(F32)
        t = i * tm + r0 + lax.broadcasted_iota(jnp.int32, (sub, 1), 0)
        for gi, w in enumerate(POOL_WINDOWS):
            sl = slice(gi * POOL_GROUP, (gi + 1) * POOL_GROUP)
            s = u_ext[:, sl]
            k = 1
            while k < w:
                s = s + pltpu.roll(s, n_ext - k, 0)
                k *= 2
            win = pltpu.roll(s, w // 2, 0)[mid]
            lo = jnp.maximum(t - w // 2, 0)
            hi = jnp.minimum(t + w - w // 2, seq_len)
            d = (win / (hi - lo).astype(F32) - u[:, sl]).astype(BF16)
            yg = jnp.dot(d, pw_ref[gi], preferred_element_type=F32)
            y_pool = _silu(pg[:, sl]) * (yg * ps_ref[:, sl])
            mix_ref[rs, D_CONV + D_ATTN + gi * POOL_GROUP:D_CONV + D_ATTN + (gi + 1) * POOL_GROUP] = (
                y_pool.astype(BF16))

        out = jnp.dot(mix_ref[rs, :], wo_ref[...], preferred_element_type=F32)
        pace = jnp.where(out[0:1, :D_CONV] < -jnp.inf, 1.0, 0.0)
        z = DEEPNORM_ALPHA * x_ref[0, rs, :] + g_ref[0] * out
        mu = jnp.mean(z, axis=-1, keepdims=True)
        zc = z - mu
        var = jnp.mean(zc * zc, axis=-1, keepdims=True)
        o_ref[0, rs, :] = zc * lax.rsqrt(var + LN_EPS) * lg_ref[...] + lb_ref[...]


def _mix(p, y_attn, x, mods3, mod_row, layer, conv_w, pool_w_bf16, pool_scale, w_out_bf16, ln_g, ln_b, tm):
    nb, t, _ = x.shape
    hb = tm // HALO
    last = t // HALO - 1
    prev = lambda i: jnp.maximum(i * hb - 1, 0)
    nxt = lambda i: jnp.minimum((i + 1) * hb, last)
    cc_blk = 2
    const2 = lambda b, i: (0, 0)
    in_specs = [
        pl.BlockSpec((1, tm, 4 * D_CONV), lambda b, i: (b, i, 0)),
        pl.BlockSpec((1, HALO, D_CONV), lambda b, i: (b, prev(i), 0)),
        pl.BlockSpec((1, HALO, D_CONV), lambda b, i: (b, prev(i), cc_blk)),
        pl.BlockSpec((1, HALO, D_CONV), lambda b, i: (b, nxt(i), 0)),
        pl.BlockSpec((1, HALO, D_CONV), lambda b, i: (b, nxt(i), cc_blk)),
        pl.BlockSpec((1, tm, 2 * D_POOL), lambda b, i: (b, i, POOL_OFF // (2 * D_POOL))),
        pl.BlockSpec((1, HALO, D_POOL), lambda b, i: (b, prev(i), POOL_OFF // D_POOL)),
        pl.BlockSpec((1, HALO, D_POOL), lambda b, i: (b, nxt(i), POOL_OFF // D_POOL)),
        pl.BlockSpec((1, tm, D_ATTN), lambda b, i: (b, i, 0)),
        pl.BlockSpec((1, tm, D_MODEL), lambda b, i: (b, i, 0)),
        pl.BlockSpec((1, 1, D_MODEL), lambda b, i: (mod_row(b), 0, 2)),
        pl.BlockSpec(conv_w.shape, const2),
        pl.BlockSpec((None,) + pool_w_bf16.shape[1:], lambda b, i: (layer, 0, 0, 0)),
        pl.BlockSpec(pool_scale.shape, const2),
        pl.BlockSpec((None,) + w_out_bf16.shape[1:], lambda b, i: (layer, 0, 0), pipeline_mode=pl.Buffered(1)),
        pl.BlockSpec(ln_g.shape, const2),
        pl.BlockSpec(ln_b.shape, const2),
    ]
    return pl.pallas_call(
        functools.partial(_mix_kernel, seq_len=t),
        grid=(nb, t // tm),
        in_specs=in_specs,
        out_specs=pl.BlockSpec((1, tm, D_MODEL), lambda b, i: (b, i, 0)),
        out_shape=jax.ShapeDtypeStruct((nb, t, D_MODEL), F32),
        scratch_shapes=[pltpu.VMEM((tm, D_MODEL), BF16)],
        compiler_params=_params("arbitrary", "arbitrary"),
        name="mix_out_ln",
    )(p, p, p, p, p, p, p, p, y_attn, x, mods3, conv_w, pool_w_bf16, pool_scale, w_out_bf16, ln_g, ln_b)


def _rope_tables(seq_len):
    n = QK_DIM // 4
    inv = ROPE_THETA ** (-np.arange(n, dtype=np.float64) / n)
    tok = np.arange(seq_len)
    pos = np.stack([tok // GRID_W, tok % GRID_W], axis=1).astype(np.float64)
    m = np.arange(LANES)
    ang = pos[:, (m // n) % 2] * inv[m % n][None, :]
    second = (m >= LANES // 2)[None, :]
    cos, sin = np.cos(ang), np.sin(ang)
    return jnp.asarray(cos, dtype=F32), jnp.asarray(np.where(second, sin, -sin), dtype=F32)


def kernel(x, c, ctx, c_ctx, w_mod, b_mod, w_in, conv_w, lam_q1, lam_k1, lam_q2, lam_k2, subln_g, pool_w,
           pool_scale, w_out, ln_g, ln_b):
    nb, seq, _ = x.shape
    ctx_len = ctx.shape[1]
    assert nb + 1 <= MOD_ROWS

    cvec = jnp.concatenate([c, c_ctx[None, :], jnp.zeros((MOD_ROWS - nb - 1, D_MODEL), F32)], axis=0)
    lamv = jnp.stack([lam_q1, lam_k1, lam_q2, lam_k2], axis=1).astype(F32)
    mods, lams = _modulation(cvec, w_mod, b_mod, lamv)
    tables = _rope_tables(seq)
    latent_row = lambda b: b
    ctx_row = lambda b: nb

    w_in_bf = _head_layout(w_in.astype(BF16))
    w_out_bf = w_out.astype(BF16)
    pool_w_bf = pool_w.astype(BF16)

    xl, xc = x, ctx
    for l in range(DEPTH):
        last = l == DEPTH - 1
        lam_init = 0.8 - 0.6 * math.exp(-0.3 * l)
        mods3 = mods[l].reshape(MOD_ROWS, 1, 3 * D_MODEL)
        gain = subln_g[l].reshape(1, V_DIM)
        p_l = _project(xl, mods3, latent_row, w_in_bf, l, 0, D_IN // PROJ_TN, PROJ_TM, tables)
        xc_flat = xc.reshape(1, nb * ctx_len, D_MODEL)
        if last:
            p_c = _project(xc_flat, mods3, ctx_row, w_in_bf, l, K_OFF // PROJ_TN, 2 * D_ATTN // PROJ_TN,
                           nb * ctx_len, None).reshape(nb, ctx_len, 2 * D_ATTN)
            kc_off, vc_off = 0, D_ATTN
        else:
            p_c = _project(xc_flat, mods3, ctx_row, w_in_bf, l, 0, D_IN // PROJ_TN, nb * ctx_len,
                           None).reshape(nb, ctx_len, D_IN)
            kc_off, vc_off = K_OFF, V_OFF
        y_attn = _attention(lams[l], gain, p_l, p_c, kc_off, vc_off, ATTN_TQ, lam_init)
        mix_args = (l, conv_w[l], pool_w_bf, pool_scale[l].reshape(1, D_POOL), w_out_bf,
                    ln_g[l].reshape(1, D_MODEL), ln_b[l].reshape(1, D_MODEL))
        xl_new = _mix(p_l, y_attn, xl, mods3, latent_row, *mix_args, MIX_TM)
        if not last:
            y_attn_c = _attention_ctx(lams[l], gain, p_c, lam_init)
            xc = _mix(p_c, y_attn_c, xc, mods3, ctx_row, *mix_args, ctx_len)
        xl = xl_new
    return xl
```
